```python
import math
import jax
import jax.numpy as jnp
from jax import lax
import numpy as np

D_MODEL = 1024
BATCH = 4
SEQ = 8192
DEPTH = 4
DEC_BATCH = 8
DEC_SEQ = 2048
PAST_LEN = 128

F32 = jnp.float32

N_MIXERS = 4
EXPAND = 2
E_WIDTH = EXPAND * D_MODEL
NORM_EPS = 1e-6
CHUNK = 64

HG_DK = 128
HG_HEADS = E_WIDTH // HG_DK
HG_DV = E_WIDTH // HG_HEADS

HY_EMB = 33
HY_BANDS = (HY_EMB - 1) // 2
HY_FH = 64
HY_INNER = 2
HY_SHORT = 3
HY_FAST_DECAY = 0.3
HY_SLOW_DECAY = 1.5
HY_TARGET = 1e-2
HY_FILTER_SCALE = 0.05

RT_HEADS = 4
RT_QK = D_MODEL
RT_DK = RT_QK // RT_HEADS
RT_DV = E_WIDTH // RT_HEADS
RT_ROPE_BASE = 10000.0

LRU_CONV = 4
LRU_BLOCKS = 16
LRU_BS = E_WIDTH // LRU_BLOCKS
LRU_C = 8.0

N_HG = len(range(0, DEPTH, N_MIXERS))
N_HY = len(range(1, DEPTH, N_MIXERS))
N_RT = len(range(2, DEPTH, N_MIXERS))
N_LRU = len(range(3, DEPTH, N_MIXERS))

kernel_name = 'hybrid_bidir_interleaved_encoder'


def _rms(x):
    return x * lax.rsqrt(jnp.mean(x * x, axis=-1, keepdims=True) + NORM_EPS)


def _flip(t):
    return jnp.flip(t, axis=1)


def _depthwise_conv(x, w, b, left):
    K, C = w.shape
    y = lax.conv_general_dilated(x, w.astype(x.dtype)[:, None, :], window_strides=(1,),
                                 padding=[(left, K - 1 - left)],
                                 dimension_numbers=('NWC', 'WIO', 'NWC'), feature_group_count=C)
    return y + b


def _to_chunks(t):
    B, L, H, d = t.shape
    return t.reshape(B, L // CHUNK, CHUNK, H, d).transpose(1, 0, 3, 2, 4)


def _from_chunks(t):
    n, B, H, C, d = t.shape
    return t.transpose(1, 0, 3, 2, 4).reshape(B, n * C, H, d)


def _gated_chunkwise(q, k, v, g):
    B, L, H, dk = q.shape
    dv = v.shape[-1]
    lower = jnp.tril(jnp.ones((CHUNK, CHUNK), dtype=bool))
    mid = CHUNK // 2

    def step(S, blk):
        qb, kb, vb, gb = blk
        b = jnp.cumsum(gb, axis=2)
        b_ref = b[:, :, mid:mid + 1]
        b_last = b[:, :, CHUNK - 1:]
        scores = jnp.einsum('bhtd,bhsd->bhts', qb * jnp.exp(b - b_ref), kb * jnp.exp(b_ref - b))
        scores = jnp.where(lower, scores, 0.0)
        o = (jnp.einsum('bhts,bhsv->bhtv', scores, vb)
             + jnp.einsum('bhtd,bhdv->bhtv', qb * jnp.exp(b), S))
        S = (jnp.exp(b_last)[:, :, 0, :, None] * S
             + jnp.einsum('bhsd,bhsv->bhdv', kb * jnp.exp(b_last - b), vb))
        return S, o

    S0 = jnp.zeros((B, H, dk, dv), F32)
    _, o = lax.scan(step, S0, (_to_chunks(q), _to_chunks(k), _to_chunks(v), _to_chunks(g)))
    return _from_chunks(o)


def _retention_chunkwise(q, k, v, log_gamma):
    B, L, H, dk = q.shape
    dv = v.shape[-1]
    pos = jnp.arange(CHUNK, dtype=F32)
    rel = pos[:, None] - pos[None, :]
    decay = jnp.where(rel >= 0, jnp.exp(log_gamma[:, None, None] * jnp.maximum(rel, 0.0)), 0.0)
    q_dec = jnp.exp(log_gamma[:, None] * (pos + 1.0))[None, :, :, None]
    k_dec = jnp.exp(log_gamma[:, None] * (CHUNK - 1.0 - pos))[None, :, :, None]
    c_dec = jnp.exp(log_gamma * CHUNK)[None, :, None, None]

    def step(R, blk):
        qb, kb, vb = blk
        scores = jnp.einsum('bhtd,bhsd->bhts', qb, kb) * decay
        o = (jnp.einsum('bhts,bhsv->bhtv', scores, vb)
             + q_dec * jnp.einsum('bhtd,bhdv->bhtv', qb, R))
        R = c_dec * R + jnp.einsum('bhsd,bhsv->bhdv', kb * k_dec, vb)
        return R, o

    R0 = jnp.zeros((B, H, dk, dv), F32)
    _, o = lax.scan(step, R0, (_to_chunks(q), _to_chunks(k), _to_chunks(v)))
    return _from_chunks(o)


def _rotary(t):
    B, L, H, d = t.shape
    inv = RT_ROPE_BASE ** (-jnp.arange(0, d, 2, dtype=F32) / d)
    ang = jnp.arange(L, dtype=F32)[:, None] * inv[None]
    cos = jnp.cos(ang)[None, :, None]
    sin = jnp.sin(ang)[None, :, None]
    t1, t2 = t[..., :d // 2], t[..., d // 2:]
    return jnp.concatenate([t1 * cos - t2 * sin, t1 * sin + t2 * cos], axis=-1)


def _hgrn2_mixer(h, lb, w_in, norm_g, w_out):
    B, L, _ = h.shape
    q, f_fw, f_bw, i, z = jnp.split(h @ w_in, 5, axis=-1)

    def heads(t):
        return t.reshape(B, L, HG_HEADS, -1)

    q = heads(jax.nn.silu(q))
    i = heads(i)

    def gate(f_raw):
        f = lb + (1.0 - lb) * jax.nn.sigmoid(f_raw)
        return heads(1.0 - f), heads(jnp.log(f))

    k_fw, g_fw = gate(f_fw)
    k_bw, g_bw = gate(f_bw)
    o = (_gated_chunkwise(q, k_fw, i, g_fw)
         + _flip(_gated_chunkwise(_flip(q), _flip(k_bw), _flip(i), _flip(g_bw))))
    o = (_rms(o) * norm_g).reshape(B, L, E_WIDTH)
    return (o * jax.nn.silu(z)) @ w_out


def _hyena_filters(L, w1, b1, w2, b2, w_out, freq):
    t = jnp.linspace(0.0, 1.0, L, dtype=F32)[:, None]
    w = 2.0 * math.pi * jnp.arange(L, dtype=F32)[:, None] / L
    bands = jnp.linspace(1e-4, HY_BANDS - 1, HY_BANDS, dtype=F32)[None]
    z = jnp.concatenate([t, jnp.cos(bands * w), -jnp.sin(bands * w)], axis=-1)
    a = jnp.sin(freq * (z @ w1 + b1))
    for j in range(HY_INNER):
        a = jnp.sin(freq * (a @ w2[j] + b2[j]))
    filt = a @ w_out
    max_decay = math.log(HY_TARGET) / HY_FAST_DECAY
    min_decay = math.log(HY_TARGET) / HY_SLOW_DECAY
    deltas = jnp.abs(jnp.linspace(min_decay, max_decay, E_WIDTH, dtype=F32))
    window = jnp.exp(-t * deltas)
    return filt[:, :E_WIDTH] * window, filt[:, E_WIDTH:] * window


def _bidir_fftconv(u, h_fw, h_bw):
    B, L, C = u.shape
    taps = jnp.concatenate([h_fw[:1] + h_bw[:1], h_fw[1:], jnp.zeros((1, C), F32), h_bw[:0:-1]], axis=0)
    u_f = jnp.fft.rfft(u, n=2 * L, axis=1)
    t_f = jnp.fft.rfft(taps, axis=0)
    return jnp.fft.irfft(u_f * t_f[None], n=2 * L, axis=1)[:, :L]


def _hyena_mixer(h, w_in, b_in, conv_w, conv_b, f_w1, f_b1, f_w2, f_b2, f_wout, f_freq, skip, w_out):
    B, L, _ = h.shape
    proj = h @ w_in + b_in
    vxx = _depthwise_conv(proj[..., :3 * E_WIDTH], conv_w, conv_b, left=HY_SHORT // 2)
    z = proj[..., 3 * E_WIDTH:]
    x0, x1, v = jnp.split(vxx, 3, axis=-1)
    h_fw, h_bw = _hyena_filters(L, f_w1, f_b1, f_w2, f_b2, f_wout, f_freq)
    u = x0 * v
    y = x1 * (_bidir_fftconv(u, h_fw, h_bw) + u * skip)
    return (y * jax.nn.silu(z)) @ w_out


def _retention_mixer(h, w_in, gn_g, w_out):
    B, L, _ = h.shape
    q, k, v, g = jnp.split(h @ w_in, [RT_QK, 2 * RT_QK, 2 * RT_QK + E_WIDTH], axis=-1)
    q = _rotary(q.reshape(B, L, RT_HEADS, RT_DK))
    k = _rotary(k.reshape(B, L, RT_HEADS, RT_DK)) * (RT_DK ** -0.5)
    v = v.reshape(B, L, RT_HEADS, RT_DV)
    head_idx = jnp.arange(RT_HEADS, dtype=F32)
    lg_fw = jnp.log1p(-jnp.exp2(-5.0 - head_idx))
    lg_bw = jnp.log1p(-jnp.exp2(-5.5 - head_idx))
    o = (_retention_chunkwise(q, k, v, lg_fw)
         + _flip(_retention_chunkwise(_flip(q), _flip(k), _flip(v), lg_bw)))
    o = _rms(o).reshape(B, L, E_WIDTH) * gn_g
    return (o * jax.nn.silu(g)) @ w_out


def _rglru_mixer(h, w_in, conv_w, conv_b, gate_w, gate_b, lam, w_out):
    B, L, _ = h.shape
    xb, z = jnp.split(h @ w_in, 2, axis=-1)
    xb = _depthwise_conv(xb, conv_w, conv_b, left=LRU_CONV // 2)
    xblk = xb.reshape(B, L, LRU_BLOCKS, LRU_BS)

    def combine(e1, e2):
        a1, b1 = e1
        a2, b2 = e2
        return a1 * a2, a2 * b1 + b2

    def direction(d, reverse):
        gates = (jnp.einsum('blnk,gnkj->gblnj', xblk, gate_w[d]).reshape(2, B, L, E_WIDTH)
                 + gate_b[d][:, None, None, :])
        r = jax.nn.sigmoid(gates[0])
        i = jax.nn.sigmoid(gates[1])
        log_a = -LRU_C * r * jax.nn.softplus(-lam[d].astype(F32))
        a = jnp.exp(log_a)
        b = jnp.sqrt(-jnp.expm1(2.0 * log_a)) * (i * xb)
        _, hs = lax.associative_scan(combine, (a, b), axis=1, reverse=reverse)
        return hs

    y = direction(0, False) + direction(1, True)
    return (y * jax.nn.silu(z)) @ w_out


def _trunk(x, c, ada_w, ada_b, norm_g, final_g, hg, hy, rt, lru):
    x = x.astype(F32)
    cs = jax.nn.silu(c.astype(F32))
    for layer in range(DEPTH):
        kind, j = layer % N_MIXERS, layer // N_MIXERS
        shift, scale, gate = jnp.split(cs @ ada_w[layer] + ada_b[layer], 3, axis=-1)
        h = _rms(x) * norm_g[layer] * (1.0 + scale[:, None]) + shift[:, None]
        if kind == 0:
            hg_lb, hg_w_in, hg_norm_g, hg_w_out = hg
            lb = jnp.cumsum(jax.nn.softmax(hg_lb.astype(F32), axis=0), axis=0)[layer]
            y = _hgrn2_mixer(h, lb, hg_w_in[j], hg_norm_g[j], hg_w_out[j])
        elif kind == 1:
            (hy_w_in, hy_b_in, hy_conv_w, hy_conv_b, hy_f_w1, hy_f_b1, hy_f_w2, hy_f_b2,
             hy_f_wout, hy_f_freq, hy_skip, hy_w_out) = hy
            y = _hyena_mixer(h, hy_w_in[j], hy_b_in[j], hy_conv_w[j], hy_conv_b[j], hy_f_w1[j],
                             hy_f_b1[j], hy_f_w2[j], hy_f_b2[j], hy_f_wout[j], hy_f_freq[j],
                             hy_skip[j], hy_w_out[j])
        elif kind == 2:
            rt_w_in, rt_gn_g, rt_w_out = rt
            y = _retention_mixer(h, rt_w_in[j], rt_gn_g[j], rt_w_out[j])
        else:
            lru_w_in, lru_conv_w, lru_conv_b, lru_gate_w, lru_gate_b, lru_lambda, lru_w_out = lru
            y = _rglru_mixer(h, lru_w_in[j], lru_conv_w[j], lru_conv_b[j], lru_gate_w[j],
                             lru_gate_b[j], lru_lambda[j], lru_w_out[j])
        x = x + gate[:, None] * y
    return _rms(x) * final_g


def setup_inputs(seed: int = 0) -> dict:
    key = jax.random.key(seed)
    ks = jax.random.split(key, 34)
    D, E = D_MODEL, E_WIDTH

    def nrm(k, shape, s):
        return s * jax.random.normal(k, shape, F32)

    a8 = jax.random.uniform(ks[32], (N_LRU, 2, E), F32, 0.9, 0.999)
    a = a8 ** (1.0 / LRU_C)
    return {
        'x_prompt': nrm(ks[0], (BATCH, SEQ, D), 1.0),
        'x_sample': nrm(ks[1], (DEC_BATCH, DEC_SEQ, D), 1.0),
        'c_prompt': nrm(ks[2], (BATCH, D), 1.0),
        'c_sample': nrm(ks[3], (DEC_BATCH, D), 1.0),
        'ada_w': nrm(ks[4], (DEPTH, D, 3 * D), 0.5 * D ** -0.5),
        'ada_b': nrm(ks[5], (DEPTH, 3 * D), 0.02),
        'norm_g': 1.0 + nrm(ks[6], (DEPTH, D), 0.02),
        'final_g': 1.0 + nrm(ks[7], (D,), 0.02),
        'hg_lb': nrm(ks[8], (DEPTH + 1, E), 0.1),
        'hg_w_in': nrm(ks[9], (N_HG, D, 5 * E), D ** -0.5),
        'hg_norm_g': 1.0 + nrm(ks[10], (N_HG, HG_DV), 0.02),
        'hg_w_out': nrm(ks[11], (N_HG, E, D), E ** -0.5),
        'hy_w_in': nrm(ks[12], (N_HY, D, 4 * E), D ** -0.5),
        'hy_b_in': nrm(ks[13], (N_HY, 4 * E), 0.02),
        'hy_conv_w': nrm(ks[14], (N_HY, HY_SHORT, 3 * E), HY_SHORT ** -0.5),
        'hy_conv_b': nrm(ks[15], (N_HY, 3 * E), 0.02),
        'hy_f_w1': nrm(ks[16], (N_HY, HY_EMB, HY_FH), HY_EMB ** -0.5),
        'hy_f_b1': nrm(ks[17], (N_HY, HY_FH), 0.1),
        'hy_f_w2': nrm(ks[18], (N_HY, HY_INNER, HY_FH, HY_FH), HY_FH ** -0.5),
        'hy_f_b2': nrm(ks[19], (N_HY, HY_INNER, HY_FH), 0.1),
        'hy_f_wout': nrm(ks[20], (N_HY, HY_FH, 2 * E), HY_FILTER_SCALE * HY_FH ** -0.5),
        'hy_f_freq': 1.0 + nrm(ks[21], (N_HY, HY_FH), 0.02),
        'hy_skip': nrm(ks[22], (N_HY, E), 0.5),
        'hy_w_out': nrm(ks[23], (N_HY, E, D), E ** -0.5),
        'rt_w_in': nrm(ks[24], (N_RT, D, 2 * RT_QK + 2 * E), D ** -0.5),
        'rt_gn_g': 1.0 + nrm(ks[25], (N_RT, E), 0.02),
        'rt_w_out': nrm(ks[26], (N_RT, E, D), E ** -0.5),
        'lru_w_in': nrm(ks[27], (N_LRU, D, 2 * E), D ** -0.5),
        'lru_conv_w': nrm(ks[28], (N_LRU, LRU_CONV, E), LRU_CONV ** -0.5),
        'lru_conv_b': nrm(ks[29], (N_LRU, E), 0.02),
        'lru_gate_w': nrm(ks[30], (N_LRU, 2, 2, LRU_BLOCKS, LRU_BS, LRU_BS), LRU_BS ** -0.5),
        'lru_gate_b': nrm(ks[31], (N_LRU, 2, 2, E), 0.02),
        'lru_lambda': jnp.log(a) - jnp.log1p(-a),
        'lru_w_out': nrm(ks[33], (N_LRU, E, D), E ** -0.5),
    }


def reference(x_prompt, x_sample, c_prompt, c_sample, ada_w, ada_b, norm_g, final_g,
              hg_lb, hg_w_in, hg_norm_g, hg_w_out,
              hy_w_in, hy_b_in, hy_conv_w, hy_conv_b, hy_f_w1, hy_f_b1, hy_f_w2, hy_f_b2,
              hy_f_wout, hy_f_freq, hy_skip, hy_w_out,
              rt_w_in, rt_gn_g, rt_w_out,
              lru_w_in, lru_conv_w, lru_conv_b, lru_gate_w, lru_gate_b, lru_lambda, lru_w_out):
    hg = (hg_lb, hg_w_in, hg_norm_g, hg_w_out)
    hy = (hy_w_in, hy_b_in, hy_conv_w, hy_conv_b, hy_f_w1, hy_f_b1, hy_f_w2, hy_f_b2,
          hy_f_wout, hy_f_freq, hy_skip, hy_w_out)
    rt = (rt_w_in, rt_gn_g, rt_w_out)
    lru = (lru_w_in, lru_conv_w, lru_conv_b, lru_gate_w, lru_gate_b, lru_lambda, lru_w_out)
    y_prompt = _trunk(x_prompt, c_prompt, ada_w, ada_b, norm_g, final_g, hg, hy, rt, lru).astype(x_prompt.dtype)
    y_sample = _trunk(x_sample, c_sample, ada_w, ada_b, norm_g, final_g, hg, hy, rt, lru).astype(x_sample.dtype)
    return (y_prompt, y_sample)
```

```python
import functools
import math

import numpy as np
import jax
import jax.numpy as jnp
from jax import lax
from jax.experimental import pallas as pl
from jax.experimental.pallas import tpu as pltpu

F32 = jnp.float32
BF16 = jnp.bfloat16

D_MODEL = 1024
DEPTH = 4
E_WIDTH = 2 * D_MODEL
NORM_EPS = 1e-6
LANES = 128
SUBLANES = 8
MIB = 1024 * 1024

HG_CHUNK = 64
HG_DK = 128
HG_HEADS = E_WIDTH // HG_DK

HY_EMB = 33
HY_BANDS = 16
HY_FH = 64
HY_INNER = 2
HY_FAST_DECAY = 0.3
HY_SLOW_DECAY = 1.5
HY_TARGET = 1e-2

RT_HEADS = 4
RT_QK = D_MODEL
RT_DK = RT_QK // RT_HEADS
RT_DV = E_WIDTH // RT_HEADS
RT_ROPE_BASE = 10000.0
RT_CHUNK = 256

LRU_CONV = 4
LRU_BLOCKS = 16
LRU_BS = E_WIDTH // LRU_BLOCKS
LRU_C = 8.0

_NT = (((1,), (1,)), ((), ()))


def _cparams(sem, vmem_mib):
    return pltpu.CompilerParams(dimension_semantics=sem, vmem_limit_bytes=vmem_mib * MIB)


def _bdot(a, b):
    return jnp.dot(a.astype(BF16), b.astype(BF16), preferred_element_type=F32)


def _bdot_nt(a, b):
    return lax.dot_general(a.astype(BF16), b.astype(BF16), _NT, preferred_element_type=F32)


def _sigmoid(x):
    return jax.nn.sigmoid(x)


def _silu(x):
    return x * _sigmoid(x)


def _expm1(x):
    u = jnp.exp(x)
    plain = jnp.logical_or(u == 1.0, x < -0.5)
    small = (u - 1.0) * x / jnp.where(plain, 1.0, jnp.log(u))
    return jnp.where(u == 1.0, x, jnp.where(x < -0.5, u - 1.0, small))


def _adaln_kernel(c_ref, w_ref, b_ref, o_ref):
    cs = _silu(c_ref[...])
    o_ref[...] = _bdot(cs, w_ref[...]) + b_ref[...]


def _adaln(c_all, ada_w16, ada_b):
    bp, d = c_all.shape
    tn = 1024
    return pl.pallas_call(
        _adaln_kernel,
        grid=(DEPTH, 3 * d // tn),
        in_specs=[pl.BlockSpec((bp, d), lambda l, j: (0, 0)),
                  pl.BlockSpec((None, d, tn), lambda l, j: (l, 0, j)),
                  pl.BlockSpec((None, 1, tn), lambda l, j: (l, 0, j))],
        out_specs=pl.BlockSpec((None, bp, tn), lambda l, j: (l, 0, j)),
        out_shape=jax.ShapeDtypeStruct((DEPTH, bp, 3 * d), F32),
        compiler_params=_cparams(("arbitrary", "arbitrary"), 32),
        name="adaln",
    )(c_all, ada_w16, ada_b.reshape(DEPTH, 1, 3 * d))


def _in_proj_kernel(x_ref, g_ref, sc_ref, sh_ref, w_ref, b_ref, o_ref, h_ref):
    @pl.when(pl.program_id(2) == 0)
    def _():
        x = x_ref[...]
        ms = jnp.mean(x * x, axis=-1, keepdims=True)
        h = x * lax.rsqrt(ms + NORM_EPS) * g_ref[...] * (1.0 + sc_ref[...]) + sh_ref[...]
        h_ref[...] = h.astype(BF16)

    o_ref[...] = jnp.dot(h_ref[...], w_ref[...], preferred_element_type=F32) + b_ref[...]


def _in_proj(x, norm_g, scale, shift, w16, bias):
    b, l, d = x.shape
    p = w16.shape[1]
    tm = min(l, 1024)
    tn = 1024
    return pl.pallas_call(
        _in_proj_kernel,
        grid=(b, l // tm, p // tn),
        in_specs=[pl.BlockSpec((None, tm, d), lambda bi, i, j: (bi, i, 0)),
                  pl.BlockSpec((1, d), lambda bi, i, j: (0, 0)),
                  pl.BlockSpec((None, 1, d), lambda bi, i, j: (bi, 0, 0)),
                  pl.BlockSpec((None, 1, d), lambda bi, i, j: (bi, 0, 0)),
                  pl.BlockSpec((d, tn), lambda bi, i, j: (0, j)),
                  pl.BlockSpec((1, tn), lambda bi, i, j: (0, j))],
        out_specs=pl.BlockSpec((None, tm, tn), lambda bi, i, j: (bi, i, j)),
        out_shape=jax.ShapeDtypeStruct((b, l, p), F32),
        scratch_shapes=[pltpu.VMEM((tm, d), BF16)],
        compiler_params=_cparams(("arbitrary", "arbitrary", "arbitrary"), 40),
        name="in_proj",
    )(x, norm_g.reshape(1, d), scale.reshape(b, 1, d), shift.reshape(b, 1, d), w16,
      bias.reshape(1, p))


def _head_rms(o, width):
    parts = []
    for s in range(0, o.shape[1], width):
        oh = o[:, s:s + width]
        ms = jnp.mean(oh * oh, axis=-1, keepdims=True)
        parts.append(oh * lax.rsqrt(ms + NORM_EPS))
    return jnp.concatenate(parts, axis=1)


def _mix_hgrn2(o_ref, z_ref, g_ref):
    return (_head_rms(o_ref[...], HG_DK) * g_ref[...]) * _silu(z_ref[...])


def _mix_hyena(yc_ref, u_ref, g1_ref, skip_ref):
    return g1_ref[...] * (yc_ref[...] + u_ref[...] * skip_ref[...])


def _mix_retention(o_ref, z_ref, g_ref):
    return (_head_rms(o_ref[...], RT_DV) * g_ref[...]) * _silu(z_ref[...])


def _mix_lru(y_ref, z_ref):
    return y_ref[...] * _silu(z_ref[...])


def _out_proj_kernel(*refs, mix, n_mix, final):
    mix_refs = refs[:n_mix]
    w_ref, x_ref, gate_ref = refs[n_mix:n_mix + 3]
    o_ref = refs[-1]
    y = mix(*mix_refs)
    out = x_ref[...] + gate_ref[...] * _bdot(y, w_ref[...])
    if final:
        fg_ref = refs[n_mix + 3]
        ms = jnp.mean(out * out, axis=-1, keepdims=True)
        out = out * lax.rsqrt(ms + NORM_EPS) * fg_ref[...]
    o_ref[...] = out


def _out_proj(mix, mix_args, mix_specs, w16, x, gate, final_g=None):
    b, l, d = x.shape
    e = w16.shape[0]
    tm = min(l, 256)
    in_specs = list(mix_specs(tm)) + [
        pl.BlockSpec((e, d), lambda bi, i: (0, 0)),
        pl.BlockSpec((None, tm, d), lambda bi, i: (bi, i, 0)),
        pl.BlockSpec((None, 1, d), lambda bi, i: (bi, 0, 0))]
    args = list(mix_args) + [w16, x, gate.reshape(b, 1, d)]
    if final_g is not None:
        in_specs.append(pl.BlockSpec((1, d), lambda bi, i: (0, 0)))
        args.append(final_g.reshape(1, d))
    return pl.pallas_call(
        functools.partial(_out_proj_kernel, mix=mix, n_mix=len(mix_args), final=final_g is not None),
        grid=(b, l // tm),
        in_specs=in_specs,
        out_specs=pl.BlockSpec((None, tm, d), lambda bi, i: (bi, i, 0)),
        out_shape=jax.ShapeDtypeStruct((b, l, d), F32),
        compiler_params=_cparams(("arbitrary", "arbitrary"), 48),
        name="out_proj",
    )(*args)


def _row_spec(tm, width, col):
    return pl.BlockSpec((None, tm, width), lambda bi, i: (bi, i, col))


def _vec_spec(width):
    return pl.BlockSpec((1, width), lambda bi, i: (0, 0))


def _hgrn2_kernel(*refs, reverse, hb, nch, add):
    q_ref, f_ref, v_ref, lb_ref = refs[:4]
    prev_ref = refs[4] if add else None
    o_ref, st_ref = refs[-2:]

    @pl.when(pl.program_id(2) == 0)
    def _():
        st_ref[...] = jnp.zeros_like(st_ref)

    lb_exp = jnp.exp(lb_ref[...] - jnp.max(lb_ref[...], axis=0, keepdims=True))
    lb_all = lb_exp[0:1, :] / jnp.sum(lb_exp, axis=0, keepdims=True)

    c = HG_CHUNK
    row = lax.broadcasted_iota(jnp.int32, (c, c), 0)
    col = lax.broadcasted_iota(jnp.int32, (c, c), 1)
    mask = (col >= row) if reverse else (col <= row)
    tri = mask.astype(BF16)
    mid = c // 2
    ref_row = (c - 1 - mid) if reverse else mid
    last_row = 0 if reverse else c - 1

    def chunk(ci, carry):
        cc = (nch - 1 - ci) if reverse else ci
        r0 = pl.multiple_of(cc * c, c)
        for hh in range(hb):
            sl = slice(hh * HG_DK, (hh + 1) * HG_DK)
            q = _silu(q_ref[pl.ds(r0, c), sl])
            lb = lb_all[:, sl]
            f = lb + (1.0 - lb) * _sigmoid(f_ref[pl.ds(r0, c), sl])
            k = 1.0 - f
            g = jnp.log(f)
            v = v_ref[pl.ds(r0, c), sl]
            g1 = g.astype(BF16)
            r1 = g - g1.astype(F32)
            g2 = r1.astype(BF16)
            g3 = (r1 - g2.astype(F32)).astype(BF16)
            bsum = (jnp.dot(tri, g1, preferred_element_type=F32)
                    + jnp.dot(tri, g2, preferred_element_type=F32)
                    + jnp.dot(tri, g3, preferred_element_type=F32))
            b_ref_row = bsum[ref_row:ref_row + 1, :]
            b_last = bsum[last_row:last_row + 1, :]
            scores = _bdot_nt(q * jnp.exp(bsum - b_ref_row), k * jnp.exp(b_ref_row - bsum))
            scores = jnp.where(mask, scores, 0.0)
            st = st_ref[hh]
            o = _bdot(scores, v) + _bdot_nt(q * jnp.exp(bsum), st)
            st_ref[hh] = st * jnp.exp(b_last) + _bdot(v.T, k * jnp.exp(b_last - bsum))
            if add:
                o = o + prev_ref[pl.ds(r0, c), sl]
            o_ref[pl.ds(r0, c), sl] = o
        return carry

    lax.fori_loop(0, nch, chunk, 0)


def _hgrn2_dir(proj, lb, reverse, prev):
    b, l, _ = proj.shape
    e = E_WIDTH
    hb = 4
    w = hb * HG_DK
    t = min(l, 512)
    nt = l // t
    ncol = e // w
    fsec = 2 if reverse else 1

    def rows(bi, h, ti):
        return (nt - 1 - ti) if reverse else ti

    def sec(s):
        return pl.BlockSpec((None, t, w), lambda bi, h, ti: (bi, rows(bi, h, ti), s * ncol + h))

    in_specs = [sec(0), sec(fsec), sec(3), pl.BlockSpec((DEPTH + 1, w), lambda bi, h, ti: (0, h))]
    args = [proj, proj, proj, lb]
    out_spec = pl.BlockSpec((None, t, w), lambda bi, h, ti: (bi, rows(bi, h, ti), h))
    if prev is not None:
        in_specs.append(out_spec)
        args.append(prev)
    return pl.pallas_call(
        functools.partial(_hgrn2_kernel, reverse=reverse, hb=hb, nch=t // HG_CHUNK,
                          add=prev is not None),
        grid=(b, ncol, nt),
        in_specs=in_specs,
        out_specs=out_spec,
        out_shape=jax.ShapeDtypeStruct((b, l, e), F32),
        scratch_shapes=[pltpu.VMEM((hb, HG_DK, HG_DK), F32)],
        compiler_params=_cparams(("arbitrary", "arbitrary", "arbitrary"), 32),
        name="hgrn2_bwd" if reverse else "hgrn2_fwd",
    )(*args)


def _ret_tables(reverse):
    c = RT_CHUNK
    hidx = np.arange(RT_HEADS, dtype=np.float64)
    lg = np.log1p(-np.exp2((-5.5 if reverse else -5.0) - hidx))[:, None]
    pos = np.arange(c, dtype=np.float64)[None, :]
    rel = pos[0][:, None] - pos[0][None, :]
    if reverse:
        rel = -rel
    decay = np.where(rel >= 0, np.exp(lg[:, :, None] * np.maximum(rel, 0.0)[None]), 0.0)
    q_dec = np.exp(lg * ((c - pos) if reverse else (pos + 1.0)))
    k_dec = np.exp(lg * (pos if reverse else (c - 1.0 - pos)))
    c_dec = np.exp(lg * c)
    return (jnp.asarray(decay, F32),
            jnp.asarray(np.broadcast_to(q_dec[:, :, None], (RT_HEADS, c, RT_DV)), F32),
            jnp.asarray(np.broadcast_to(k_dec[:, :, None], (RT_HEADS, c, RT_DK)), F32),
            jnp.asarray(np.broadcast_to(c_dec[:, :, None], (RT_HEADS, 1, RT_DV)), F32))


def _rope_tables(l):
    inv = RT_ROPE_BASE ** (-jnp.arange(0, RT_DK, 2, dtype=F32) / RT_DK)
    ang = jnp.arange(l, dtype=F32)[:, None] * inv[None]
    return jnp.cos(ang), jnp.sin(ang)


def _ret_kernel(*refs, add):
    q_ref, k_ref, v_ref, cos_ref, sin_ref, dec_ref, qd_ref, kd_ref, cd_ref = refs[:9]
    prev_ref = refs[9] if add else None
    o_ref, r_ref = refs[-2:]

    @pl.when(pl.program_id(2) == 0)
    def _():
        r_ref[...] = jnp.zeros_like(r_ref)

    cos = cos_ref[...]
    sin = sin_ref[...]
    half = RT_DK // 2

    def rot(t):
        t1 = t[:, :half]
        t2 = t[:, half:]
        return jnp.concatenate([t1 * cos - t2 * sin, t1 * sin + t2 * cos], axis=1)

    q = rot(q_ref[...])
    k = rot(k_ref[...]) * (RT_DK ** -0.5)
    v = v_ref[...]
    scores = _bdot_nt(q, k) * dec_ref[...]
    r = r_ref[...]
    o = _bdot(scores, v) + qd_ref[...] * _bdot(q, r)
    r_ref[...] = cd_ref[...] * r + _bdot((k * kd_ref[...]).T, v)
    if add:
        o = o + prev_ref[...]
    o_ref[...] = o


def _ret_dir(proj, cos, sin, reverse, prev):
    b, l, _ = proj.shape
    c = RT_CHUNK
    nt = l // c
    dec, qd, kd, cd = _ret_tables(reverse)

    def rows(ti):
        return (nt - 1 - ti) if reverse else ti

    in_specs = [
        pl.BlockSpec((None, c, RT_DK), lambda bi, h, ti: (bi, rows(ti), h)),
        pl.BlockSpec((None, c, RT_DK), lambda bi, h, ti: (bi, rows(ti), RT_HEADS + h)),
        pl.BlockSpec((None, c, RT_DV), lambda bi, h, ti: (bi, rows(ti), 2 * RT_QK // RT_DV + h)),
        pl.BlockSpec((c, RT_DK // 2), lambda bi, h, ti: (rows(ti), 0)),
        pl.BlockSpec((c, RT_DK // 2), lambda bi, h, ti: (rows(ti), 0)),
        pl.BlockSpec((None, c, c), lambda bi, h, ti: (h, 0, 0)),
        pl.BlockSpec((None, c, RT_DV), lambda bi, h, ti: (h, 0, 0)),
        pl.BlockSpec((None, c, RT_DK), lambda bi, h, ti: (h, 0, 0)),
        pl.BlockSpec((None, 1, RT_DV), lambda bi, h, ti: (h, 0, 0)),
    ]
    args = [proj, proj, proj, cos, sin, dec, qd, kd, cd]
    out_spec = pl.BlockSpec((None, c, RT_DV), lambda bi, h, ti: (bi, rows(ti), h))
    if prev is not None:
        in_specs.append(out_spec)
        args.append(prev)
    return pl.pallas_call(
        functools.partial(_ret_kernel, add=prev is not None),
        grid=(b, RT_HEADS, nt),
        in_specs=in_specs,
        out_specs=out_spec,
        out_shape=jax.ShapeDtypeStruct((b, l, E_WIDTH), F32),
        scratch_shapes=[pltpu.VMEM((RT_DK, RT_DV), F32)],
        compiler_params=_cparams(("arbitrary", "arbitrary", "arbitrary"), 32),
        name="ret_bwd" if reverse else "ret_fwd",
    )(*args)


def _halo_specs(t, w, l, col, order):
    per = t // SUBLANES
    nblk = l // SUBLANES
    prev = pl.BlockSpec((None, SUBLANES, w),
                        lambda *g: (g[0], jnp.maximum(order(*g) * per - 1, 0), col(*g)))
    nxt = pl.BlockSpec((None, SUBLANES, w),
                       lambda *g: (g[0], jnp.minimum((order(*g) + 1) * per, nblk - 1), col(*g)))
    return prev, nxt


def _fill_ext(ext_ref, x_ref, xp_ref, xn_ref, first, last, t):
    ext_ref[0:SUBLANES, :] = jnp.where(first, 0.0, xp_ref[...])
    ext_ref[SUBLANES:SUBLANES + t, :] = x_ref[...]
    ext_ref[SUBLANES + t:2 * SUBLANES + t, :] = jnp.where(last, 0.0, xn_ref[...])


def _lru_kernel(*refs, reverse, t, nt, add):
    x_ref, xp_ref, xn_ref, cw_ref, cb_ref, gw_ref, gb_ref, lam_ref = refs[:8]
    prev_ref = refs[8] if add else None
    o_ref, ext_ref, carry_ref = refs[-3:]
    ti = pl.program_id(1)
    te = (nt - 1 - ti) if reverse else ti

    @pl.when(ti == 0)
    def _():
        carry_ref[...] = jnp.zeros_like(carry_ref)

    _fill_ext(ext_ref, x_ref, xp_ref, xn_ref, te == 0, te == nt - 1, t)
    left = LRU_CONV // 2
    xb = cb_ref[...]
    for j in range(LRU_CONV):
        xb = xb + cw_ref[j:j + 1, :] * ext_ref[pl.ds(SUBLANES - left + j, t), :]

    neg_lam = -lam_ref[...]
    softplus = jnp.maximum(neg_lam, 0.0) + jnp.log1p(jnp.exp(-jnp.abs(neg_lam)))
    rowi = lax.broadcasted_iota(jnp.int32, (t, LRU_BS), 0)

    for n in range(LRU_BLOCKS):
        sl = slice(n * LRU_BS, (n + 1) * LRU_BS)
        xn = xb[:, sl]
        gates = _bdot(xn, gw_ref[n]) + gb_ref[n]
        r = _sigmoid(gates[:, :LRU_BS])
        i = _sigmoid(gates[:, LRU_BS:])
        log_a = -LRU_C * r * softplus[:, sl]
        a = jnp.exp(log_a)
        bb = jnp.sqrt(-_expm1(2.0 * log_a)) * (i * xn)
        s = 1
        while s < t:
            shift = (t - s) if reverse else s
            valid = (rowi < t - s) if reverse else (rowi >= s)
            a_sh = pltpu.roll(a, shift, 0)
            b_sh = pltpu.roll(bb, shift, 0)
            bb = jnp.where(valid, a * b_sh + bb, bb)
            a = jnp.where(valid, a * a_sh, a)
            s *= 2
        h = bb + a * carry_ref[:, sl]
        edge = 0 if reverse else t - 1
        carry_ref[:, sl] = h[edge:edge + 1, :]
        if add:
            h = h + prev_ref[:, sl]
        o_ref[:, sl] = h


def _lru_dir(proj, conv_w, conv_b, gate_w16, gate_b, lam, reverse, prev):
    b, l, _ = proj.shape
    e = E_WIDTH
    t = min(l, 256)
    nt = l // t

    def order(bi, ti):
        return (nt - 1 - ti) if reverse else ti

    xp_spec, xn_spec = _halo_specs(t, e, l, lambda bi, ti: 0, order)
    in_specs = [
        pl.BlockSpec((None, t, e), lambda bi, ti: (bi, order(bi, ti), 0)), xp_spec, xn_spec,
        pl.BlockSpec((LRU_CONV, e), lambda bi, ti: (0, 0)),
        pl.BlockSpec((1, e), lambda bi, ti: (0, 0)),
        pl.BlockSpec((LRU_BLOCKS, LRU_BS, 2 * LRU_BS), lambda bi, ti: (0, 0, 0)),
        pl.BlockSpec((LRU_BLOCKS, 1, 2 * LRU_BS), lambda bi, ti: (0, 0, 0)),
        pl.BlockSpec((1, e), lambda bi, ti: (0, 0)),
    ]
    args = [proj, proj, proj, conv_w, conv_b.reshape(1, e), gate_w16, gate_b, lam.reshape(1, e)]
    out_spec = pl.BlockSpec((None, t, e), lambda bi, ti: (bi, order(bi, ti), 0))
    if prev is not None:
        in_specs.append(out_spec)
        args.append(prev)
    return pl.pallas_call(
        functools.partial(_lru_kernel, reverse=reverse, t=t, nt=nt, add=prev is not None),
        grid=(b, nt),
        in_specs=in_specs,
        out_specs=out_spec,
        out_shape=jax.ShapeDtypeStruct((b, l, e), F32),
        scratch_shapes=[pltpu.VMEM((t + 2 * SUBLANES, e), F32), pltpu.VMEM((1, e), F32)],
        compiler_params=_cparams(("arbitrary", "arbitrary"), 48),
        name="lru_bwd" if reverse else "lru_fwd",
    )(*args)


def _hy_pre_kernel(x0_ref, x0p_ref, x0n_ref, x1_ref, x1p_ref, x1n_ref, v_ref, vp_ref, vn_ref,
                   z_ref, w0_ref, w1_ref, wv_ref, b0_ref, b1_ref, bv_ref,
                   u_ref, g1_ref, e0_ref, e1_ref, ev_ref, *, t, nt):
    ti = pl.program_id(1)
    first = ti == 0
    last = ti == nt - 1

    def conv(x_ref, xp_ref, xn_ref, ext_ref, w_ref, b_ref):
        _fill_ext(ext_ref, x_ref, xp_ref, xn_ref, first, last, t)
        out = b_ref[...]
        for j in range(3):
            out = out + w_ref[j:j + 1, :] * ext_ref[pl.ds(SUBLANES - 1 + j, t), :]
        return out

    x0 = conv(x0_ref, x0p_ref, x0n_ref, e0_ref, w0_ref, b0_ref)
    x1 = conv(x1_ref, x1p_ref, x1n_ref, e1_ref, w1_ref, b1_ref)
    v = conv(v_ref, vp_ref, vn_ref, ev_ref, wv_ref, bv_ref)
    u_ref[...] = x0 * v
    g1_ref[...] = x1 * _silu(z_ref[...])


def _hy_pre(proj, conv_w, conv_b):
    b, l, _ = proj.shape
    e = E_WIDTH
    w = 512
    t = min(l, 512)
    nt = l // t
    ncol = e // w

    def order(bi, ti, j):
        return ti

    specs, args = [], []
    for s in range(3):
        col = (lambda s: lambda bi, ti, j: s * ncol + j)(s)
        xp, xn = _halo_specs(t, w, l, col, order)
        specs += [pl.BlockSpec((None, t, w), (lambda col: lambda bi, ti, j: (bi, ti, col(bi, ti, j)))(col)),
                  xp, xn]
        args += [proj, proj, proj]
    specs.append(pl.BlockSpec((None, t, w), lambda bi, ti, j: (bi, ti, 3 * ncol + j)))
    args.append(proj)
    for s in range(3):
        specs.append(pl.BlockSpec((3, w), (lambda s: lambda bi, ti, j: (0, s * ncol + j))(s)))
        args.append(conv_w)
    cb = conv_b.reshape(1, 3 * e)
    for s in range(3):
        specs.append(pl.BlockSpec((1, w), (lambda s: lambda bi, ti, j: (0, s * ncol + j))(s)))
        args.append(cb)
    out_spec = pl.BlockSpec((None, t, w), lambda bi, ti, j: (bi, ti, j))
    return pl.pallas_call(
        functools.partial(_hy_pre_kernel, t=t, nt=nt),
        grid=(b, nt, ncol),
        in_specs=specs,
        out_specs=[out_spec, out_spec],
        out_shape=[jax.ShapeDtypeStruct((b, l, e), F32)] * 2,
        scratch_shapes=[pltpu.VMEM((t + 2 * SUBLANES, w), F32)] * 3,
        compiler_params=_cparams(("arbitrary", "arbitrary", "arbitrary"), 48),
        name="hy_pre",
    )(*args)


def _hy_filter_kernel(z_ref, w1_ref, b1_ref, w2_ref, b2_ref, fr_ref, wf_ref, wb_ref, dl_ref,
                      sd_ref, a_ref):
    @pl.when(pl.program_id(1) == 0)
    def _():
        fr = fr_ref[...]
        a = jnp.sin(fr * (_bdot(z_ref[...], w1_ref[...]) + b1_ref[...]))
        for j in range(HY_INNER):
            a = jnp.sin(fr * (_bdot(a, w2_ref[j]) + b2_ref[j]))
        a_ref[...] = a

    a = a_ref[...]
    window = jnp.exp(-z_ref[:, 0:1] * dl_ref[...])
    h_fw = _bdot(a, wf_ref[...]) * window
    h_bw = _bdot(a, wb_ref[...]) * window
    sd_ref[0] = h_fw + h_bw
    sd_ref[1] = h_fw - h_bw


def _hy_filters(l, w1, b1, w2, b2, wout16, freq):
    e = E_WIDTH
    kp = LANES
    t = jnp.linspace(0.0, 1.0, l, dtype=F32)[:, None]
    wv = 2.0 * math.pi * jnp.arange(l, dtype=F32)[:, None] / l
    bands = jnp.linspace(1e-4, HY_BANDS - 1, HY_BANDS, dtype=F32)[None]
    z = jnp.concatenate([t, jnp.cos(bands * wv), -jnp.sin(bands * wv),
                         jnp.zeros((l, kp - HY_EMB), F32)], axis=-1)
    w1p = jnp.concatenate([w1, jnp.zeros((kp - HY_EMB, HY_FH), F32)], axis=0)
    max_decay = math.log(HY_TARGET) / HY_FAST_DECAY
    min_decay = math.log(HY_TARGET) / HY_SLOW_DECAY
    deltas = jnp.abs(jnp.linspace(min_decay, max_decay, e, dtype=F32))[None]
    tm = min(l, 512)
    w = 512
    ncol = e // w
    return pl.pallas_call(
        _hy_filter_kernel,
        grid=(l // tm, ncol),
        in_specs=[pl.BlockSpec((tm, kp), lambda i, j: (i, 0)),
                  pl.BlockSpec((kp, HY_FH), lambda i, j: (0, 0)),
                  pl.BlockSpec((1, HY_FH), lambda i, j: (0, 0)),
                  pl.BlockSpec((HY_INNER, HY_FH, HY_FH), lambda i, j: (0, 0, 0)),
                  pl.BlockSpec((HY_INNER, 1, HY_FH), lambda i, j: (0, 0, 0)),
                  pl.BlockSpec((1, HY_FH), lambda i, j: (0, 0)),
                  pl.BlockSpec((HY_FH, w), lambda i, j: (0, j)),
                  pl.BlockSpec((HY_FH, w), lambda i, j: (0, ncol + j)),
                  pl.BlockSpec((1, w), lambda i, j: (0, j))],
        out_specs=pl.BlockSpec((2, tm, w), lambda i, j: (0, i, j)),
        out_shape=jax.ShapeDtypeStruct((2, l, e), F32),
        scratch_shapes=[pltpu.VMEM((tm, HY_FH), F32)],
        compiler_params=_cparams(("arbitrary", "arbitrary"), 32),
        name="hy_filter",
    )(z, w1p, b1.reshape(1, HY_FH), w2, b2.reshape(HY_INNER, 1, HY_FH), freq.reshape(1, HY_FH),
      wout16, wout16, deltas)


def _fft_dims(l):
    n = 2 * l
    n1 = int(round(math.sqrt(n)))
    assert n1 * n1 == n and n1 % 16 == 0, "sequence length must give a square DFT factorisation"
    return n1, n1


def _fft_pitch(n1):
    return 2 * n1 + SUBLANES


def _fft_tables(l):
    n1, n2 = _fft_dims(l)
    n = n1 * n2
    k1 = np.arange(n1)[:, None]
    m1 = np.arange(n1 // 2)[None, :]
    j2 = np.arange(n2)[:, None, None]
    ang = -2.0 * np.pi * (k1 * m1 / n1)[None] - 2.0 * np.pi * (j2 * k1[None] / n)
    g_fwd = np.concatenate([np.cos(ang), np.sin(ang)], axis=1)
    ang_i = 2.0 * np.pi * (m1.T * k1.T / n1)[None] + 2.0 * np.pi * (j2 * k1.T[None] / n)
    g_inv = np.concatenate([np.cos(ang_i), -np.sin(ang_i)], axis=2) / n
    a2 = -2.0 * np.pi * np.arange(n2)[:, None] * np.arange(n2)[None, :] / n2
    fr, fi = np.cos(a2), np.sin(a2)
    f2 = np.block([[fr, -fi], [fi, fr]])
    f2_inv = np.block([[fr, fi], [-fi, fr]])
    f2_half = np.stack([np.concatenate([fr, -fi], axis=1), np.concatenate([fi, fr], axis=1)])
    as16 = lambda a: jnp.asarray(a, F32).astype(BF16)
    return as16(g_fwd), as16(g_inv), as16(f2), as16(f2_inv), as16(f2_half)


def _fft_stage1(x_ref, g_ref, work_ref, n2_lo, cnt, n1, n2, pitch):
    def body(j, carry):
        jj = n2_lo + j
        xs = x_ref[pl.ds(jj, n1 // 2, stride=n2), :]
        r0 = pl.multiple_of(jj * pitch, SUBLANES)
        work_ref[pl.ds(r0, 2 * n1), :] = _bdot(g_ref[j], xs)
        return carry
    lax.fori_loop(0, cnt, body, 0)


def _fft_load_k1(work_ref, k1, n1, n2, pitch):
    br = work_ref[pl.ds(k1, n2, stride=pitch), :]
    bi = work_ref[pl.ds(n1 + k1, n2, stride=pitch), :]
    return jnp.concatenate([br, bi], axis=0)


def _hy_spec_kernel(x_ref, g_ref, f2_ref, t_ref, work_ref, *, n1, n2, nc):
    p = pl.program_id(2)
    pitch = _fft_pitch(n1)
    c2 = n2 // nc
    c1 = n1 // nc

    @pl.when(p < nc)
    def _():
        _fft_stage1(x_ref, g_ref, work_ref, p * c2, c2, n1, n2, pitch)

    @pl.when(p >= nc)
    def _():
        def body(j, carry):
            k1 = (p - nc) * c1 + j
            t_ref[j] = _bdot(f2_ref[...], _fft_load_k1(work_ref, k1, n1, n2, pitch))
            return carry
        lax.fori_loop(0, c1, body, 0)


def _fft_nc(l):
    return 8 if l >= 8192 else (2 if l >= 2048 else 1)


def _hy_spectrum(sd, tables):
    _, l, e = sd.shape
    n1, n2 = _fft_dims(l)
    nc = _fft_nc(l)
    g_fwd, _, _, _, f2_half = tables
    pitch = _fft_pitch(n1)
    return pl.pallas_call(
        functools.partial(_hy_spec_kernel, n1=n1, n2=n2, nc=nc),
        grid=(e // LANES, 2, 2 * nc),
        in_specs=[pl.BlockSpec((None, l, LANES), lambda c, j, p: (j, 0, c)),
                  pl.BlockSpec((n2 // nc, 2 * n1, n1 // 2), lambda c, j, p: (jnp.minimum(p, nc - 1), 0, 0)),
                  pl.BlockSpec((None, n2, 2 * n2), lambda c, j, p: (j, 0, 0))],
        out_specs=pl.BlockSpec((n1 // nc, n2, LANES),
                               lambda c, j, p: (jnp.maximum(p - nc, 0), j, c)),
        out_shape=jax.ShapeDtypeStruct((n1, 2 * n2, e), F32),
        scratch_shapes=[pltpu.VMEM((n2 * pitch, LANES), F32)],
        compiler_params=_cparams(("arbitrary", "arbitrary", "arbitrary"), 48),
        name="hy_spectrum",
    )(sd, g_fwd, f2_half)


def _hy_conv_kernel(u_ref, gf_ref, f2_ref, f2i_ref, t_ref, gi_ref, y_ref, work_ref, *, n1, n2, nc):
    p = pl.program_id(2)
    pitch = _fft_pitch(n1)
    c2 = n2 // nc
    c1 = n1 // nc

    @pl.when(p < nc)
    def _():
        _fft_stage1(u_ref, gf_ref, work_ref, p * c2, c2, n1, n2, pitch)

    @pl.when(jnp.logical_and(p >= nc, p < 2 * nc))
    def _():
        def body(j, carry):
            k1 = (p - nc) * c1 + j
            x = _bdot(f2_ref[...], _fft_load_k1(work_ref, k1, n1, n2, pitch))
            xr, xi = x[:n2], x[n2:]
            tr, ti = t_ref[j, :n2, :], t_ref[j, n2:, :]
            z = jnp.concatenate([xr * tr - xi * ti, xr * ti + xi * tr], axis=0)
            cmat = _bdot(f2i_ref[...], z)
            work_ref[pl.ds(k1, n2, stride=pitch), :] = cmat[:n2]
            work_ref[pl.ds(n1 + k1, n2, stride=pitch), :] = cmat[n2:]
            return carry
        lax.fori_loop(0, c1, body, 0)

    @pl.when(p >= 2 * nc)
    def _():
        def body(j, carry):
            jj = (p - 2 * nc) * c2 + j
            r0 = pl.multiple_of(jj * pitch, SUBLANES)
            d = work_ref[pl.ds(r0, 2 * n1), :]
            y_ref[pl.ds(jj, n1 // 2, stride=n2), :] = _bdot(gi_ref[j], d)
            return carry
        lax.fori_loop(0, c2, body, 0)


def _hy_conv(u, spec, tables):
    b, l, e = u.shape
    n1, n2 = _fft_dims(l)
    nc = _fft_nc(l)
    g_fwd, g_inv, f2, f2_inv, _ = tables
    pitch = _fft_pitch(n1)
    clip = lambda v: jnp.clip(v, 0, nc - 1)
    return pl.pallas_call(
        functools.partial(_hy_conv_kernel, n1=n1, n2=n2, nc=nc),
        grid=(e // LANES, b, 3 * nc),
        in_specs=[pl.BlockSpec((None, l, LANES), lambda c, bi, p: (bi, 0, c)),
                  pl.BlockSpec((n2 // nc, 2 * n1, n1 // 2), lambda c, bi, p: (clip(p), 0, 0)),
                  pl.BlockSpec((2 * n2, 2 * n2), lambda c, bi, p: (0, 0)),
                  pl.BlockSpec((2 * n2, 2 * n2), lambda c, bi, p: (0, 0)),
                  pl.BlockSpec((n1 // nc, 2 * n2, LANES), lambda c, bi, p: (clip(p - nc), 0, c)),
                  pl.BlockSpec((n2 // nc, n1 // 2, 2 * n1), lambda c, bi, p: (clip(p - 2 * nc), 0, 0))],
        out_specs=pl.BlockSpec((None, l, LANES), lambda c, bi, p: (bi, 0, c)),
        out_shape=jax.ShapeDtypeStruct((b, l, e), F32),
        scratch_shapes=[pltpu.VMEM((n2 * pitch, LANES), F32)],
        compiler_params=_cparams(("arbitrary", "arbitrary", "arbitrary"), 52),
        name="hy_conv",
    )(u, g_fwd, f2, f2_inv, spec, g_inv)


def _prep_weights(p):
    c16 = lambda a: a.astype(BF16)
    gw = p['lru_gate_w'][0]
    gw = jnp.concatenate([gw[:, 0], gw[:, 1]], axis=-1)
    gb = p['lru_gate_b'][0].reshape(2, 2, LRU_BLOCKS, 1, LRU_BS)
    gb = jnp.concatenate([gb[:, 0], gb[:, 1]], axis=-1)
    lb = p['hg_lb'].astype(F32)
    return dict(
        ada_w=c16(p['ada_w']), hg_w_in=c16(p['hg_w_in'][0]), hg_w_out=c16(p['hg_w_out'][0]),
        hy_w_in=c16(p['hy_w_in'][0]), hy_w_out=c16(p['hy_w_out'][0]), hy_f_wout=c16(p['hy_f_wout'][0]),
        rt_w_in=c16(p['rt_w_in'][0]), rt_w_out=c16(p['rt_w_out'][0]),
        lru_w_in=c16(p['lru_w_in'][0]), lru_w_out=c16(p['lru_w_out'][0]),
        lru_gate_w=c16(gw), lru_gate_b=gb, hg_lb=lb,
        hg_norm_g=jnp.tile(p['hg_norm_g'][0], HG_HEADS))


def _trunk(x, mod, p, w):
    b, l, d = x.shape
    e = E_WIDTH
    x = x.astype(F32)
    zero_bias = lambda n: jnp.zeros((n,), F32)

    def split(layer):
        m = mod[layer]
        return m[:, :d], m[:, d:2 * d], m[:, 2 * d:]

    shift, scale, gate = split(0)
    proj = _in_proj(x, p['norm_g'][0], scale, shift, w['hg_w_in'], zero_bias(5 * e))
    o = _hgrn2_dir(proj, w['hg_lb'], False, None)
    o = _hgrn2_dir(proj, w['hg_lb'], True, o)
    x = _out_proj(_mix_hgrn2, [o, proj, w['hg_norm_g'].reshape(1, e)],
                  lambda tm: [_row_spec(tm, e, 0), _row_spec(tm, e, 4), _vec_spec(e)],
                  w['hg_w_out'], x, gate)

    shift, scale, gate = split(1)
    proj = _in_proj(x, p['norm_g'][1], scale, shift, w['hy_w_in'], p['hy_b_in'][0])
    u, g1 = _hy_pre(proj, p['hy_conv_w'][0], p['hy_conv_b'][0])
    tables = _fft_tables(l)
    sd = _hy_filters(l, p['hy_f_w1'][0], p['hy_f_b1'][0], p['hy_f_w2'][0], p['hy_f_b2'][0],
                     w['hy_f_wout'], p['hy_f_freq'][0])
    spec = _hy_spectrum(sd, tables)
    yc = _hy_conv(u, spec, tables)
    x = _out_proj(_mix_hyena, [yc, u, g1, p['hy_skip'][0].reshape(1, e)],
                  lambda tm: [_row_spec(tm, e, 0)] * 3 + [_vec_spec(e)],
                  w['hy_w_out'], x, gate)

    shift, scale, gate = split(2)
    proj = _in_proj(x, p['norm_g'][2], scale, shift, w['rt_w_in'], zero_bias(2 * RT_QK + 2 * e))
    cos, sin = _rope_tables(l)
    o = _ret_dir(proj, cos, sin, False, None)
    o = _ret_dir(proj, cos, sin, True, o)
    x = _out_proj(_mix_retention, [o, proj, p['rt_gn_g'][0].reshape(1, e)],
                  lambda tm: [_row_spec(tm, e, 0), _row_spec(tm, e, 2), _vec_spec(e)],
                  w['rt_w_out'], x, gate)

    shift, scale, gate = split(3)
    proj = _in_proj(x, p['norm_g'][3], scale, shift, w['lru_w_in'], zero_bias(2 * e))
    y = None
    for dirn in range(2):
        y = _lru_dir(proj, p['lru_conv_w'][0], p['lru_conv_b'][0], w['lru_gate_w'][dirn],
                     w['lru_gate_b'][dirn], p['lru_lambda'][0][dirn], dirn == 1, y)
    return _out_proj(_mix_lru, [y, proj], lambda tm: [_row_spec(tm, e, 0), _row_spec(tm, e, 1)],
                     w['lru_w_out'], x, gate, final_g=p['final_g'])


def kernel(x_prompt, x_sample, c_prompt, c_sample, ada_w, ada_b, norm_g, final_g, hg_lb, hg_w_in, hg_norm_g, hg_w_out, hy_w_in, hy_b_in, hy_conv_w, hy_conv_b, hy_f_w1, hy_f_b1, hy_f_w2, hy_f_b2, hy_f_wout, hy_f_freq, hy_skip, hy_w_out, rt_w_in, rt_gn_g, rt_w_out, lru_w_in, lru_conv_w, lru_conv_b, lru_gate_w, lru_gate_b, lru_lambda, lru_w_out):
    p = dict(ada_w=ada_w, ada_b=ada_b, norm_g=norm_g, final_g=final_g, hg_lb=hg_lb, hg_w_in=hg_w_in,
             hg_norm_g=hg_norm_g, hg_w_out=hg_w_out, hy_w_in=hy_w_in, hy_b_in=hy_b_in,
             hy_conv_w=hy_conv_w, hy_conv_b=hy_conv_b, hy_f_w1=hy_f_w1, hy_f_b1=hy_f_b1,
             hy_f_w2=hy_f_w2, hy_f_b2=hy_f_b2, hy_f_wout=hy_f_wout, hy_f_freq=hy_f_freq,
             hy_skip=hy_skip, hy_w_out=hy_w_out, rt_w_in=rt_w_in, rt_gn_g=rt_gn_g, rt_w_out=rt_w_out,
             lru_w_in=lru_w_in, lru_conv_w=lru_conv_w, lru_conv_b=lru_conv_b, lru_gate_w=lru_gate_w,
             lru_gate_b=lru_gate_b, lru_lambda=lru_lambda, lru_w_out=lru_w_out)
    w = _prep_weights(p)
    bp, bs = c_prompt.shape[0], c_sample.shape[0]
    rows = -(-(bp + bs) // SUBLANES) * SUBLANES
    c_all = jnp.concatenate([c_prompt, c_sample, jnp.zeros((rows - bp - bs, D_MODEL), F32)], axis=0)
    mod = _adaln(c_all.astype(F32), w['ada_w'], ada_b)
    y_prompt = _trunk(x_prompt, mod[:, :bp], p, w).astype(x_prompt.dtype)
    y_sample = _trunk(x_sample, mod[:, bp:bp + bs], p, w).astype(x_sample.dtype)
    return (y_prompt, y_sample)
```

```python
import functools
import math

import numpy as np
import jax
import jax.numpy as jnp
from jax import lax
from jax.experimental import pallas as pl
from jax.experimental.pallas import tpu as pltpu

F32 = jnp.float32
BF16 = jnp.bfloat16

D_MODEL = 1024
DEPTH = 4
E_WIDTH = 2 * D_MODEL
NORM_EPS = 1e-6
LANES = 128
SUBLANES = 8
MIB = 1024 * 1024

HG_CHUNK = 64
HG_DK = 128
HG_HEADS = E_WIDTH // HG_DK

HY_EMB = 33
HY_BANDS = 16
HY_FH = 64
HY_INNER = 2
HY_FAST_DECAY = 0.3
HY_SLOW_DECAY = 1.5
HY_TARGET = 1e-2

RT_HEADS = 4
RT_QK = D_MODEL
RT_DK = RT_QK // RT_HEADS
RT_DV = E_WIDTH // RT_HEADS
RT_ROPE_BASE = 10000.0
RT_CHUNK = 256

LRU_CONV = 4
LRU_BLOCKS = 16
LRU_BS = E_WIDTH // LRU_BLOCKS
LRU_C = 8.0

_NT = (((1,), (1,)), ((), ()))


def _cparams(sem, vmem_mib):
    return pltpu.CompilerParams(dimension_semantics=sem, vmem_limit_bytes=vmem_mib * MIB)


def _bdot(a, b):
    return jnp.dot(a.astype(BF16), b.astype(BF16), preferred_element_type=F32)


def _bdot_nt(a, b):
    return lax.dot_general(a.astype(BF16), b.astype(BF16), _NT, preferred_element_type=F32)


def _sigmoid(x):
    return jax.nn.sigmoid(x)


def _silu(x):
    return x * _sigmoid(x)


def _expm1(x):
    u = jnp.exp(x)
    plain = jnp.logical_or(u == 1.0, x < -0.5)
    small = (u - 1.0) * x / jnp.where(plain, 1.0, jnp.log(u))
    return jnp.where(u == 1.0, x, jnp.where(x < -0.5, u - 1.0, small))


def _adaln_kernel(c_ref, w_ref, b_ref, o_ref):
    cs = _silu(c_ref[...])
    o_ref[...] = _bdot(cs, w_ref[...]) + b_ref[...]


def _adaln(c_all, ada_w16, ada_b):
    bp, d = c_all.shape
    tn = 1024
    return pl.pallas_call(
        _adaln_kernel,
        grid=(DEPTH, 3 * d // tn),
        in_specs=[pl.BlockSpec((bp, d), lambda l, j: (0, 0)),
                  pl.BlockSpec((None, d, tn), lambda l, j: (l, 0, j)),
                  pl.BlockSpec((None, 1, tn), lambda l, j: (l, 0, j))],
        out_specs=pl.BlockSpec((None, bp, tn), lambda l, j: (l, 0, j)),
        out_shape=jax.ShapeDtypeStruct((DEPTH, bp, 3 * d), F32),
        compiler_params=_cparams(("arbitrary", "arbitrary"), 32),
        name="adaln",
    )(c_all, ada_w16, ada_b.reshape(DEPTH, 1, 3 * d))


def _in_proj_kernel(x_ref, g_ref, sc_ref, sh_ref, w_ref, b_ref, o_ref, h_ref):
    @pl.when(pl.program_id(2) == 0)
    def _():
        x = x_ref[...]
        ms = jnp.mean(x * x, axis=-1, keepdims=True)
        h = x * lax.rsqrt(ms + NORM_EPS) * g_ref[...] * (1.0 + sc_ref[...]) + sh_ref[...]
        h_ref[...] = h.astype(BF16)

    o_ref[...] = jnp.dot(h_ref[...], w_ref[...], preferred_element_type=F32) + b_ref[...]


def _in_proj(x, norm_g, scale, shift, w16, bias):
    b, l, d = x.shape
    p = w16.shape[1]
    tm = min(l, 1024)
    tn = 1024
    return pl.pallas_call(
        _in_proj_kernel,
        grid=(b, l // tm, p // tn),
        in_specs=[pl.BlockSpec((None, tm, d), lambda bi, i, j: (bi, i, 0)),
                  pl.BlockSpec((1, d), lambda bi, i, j: (0, 0)),
                  pl.BlockSpec((None, 1, d), lambda bi, i, j: (bi, 0, 0)),
                  pl.BlockSpec((None, 1, d), lambda bi, i, j: (bi, 0, 0)),
                  pl.BlockSpec((d, tn), lambda bi, i, j: (0, j)),
                  pl.BlockSpec((1, tn), lambda bi, i, j: (0, j))],
        out_specs=pl.BlockSpec((None, tm, tn), lambda bi, i, j: (bi, i, j)),
        out_shape=jax.ShapeDtypeStruct((b, l, p), F32),
        scratch_shapes=[pltpu.VMEM((tm, d), BF16)],
        compiler_params=_cparams(("arbitrary", "arbitrary", "arbitrary"), 40),
        name="in_proj",
    )(x, norm_g.reshape(1, d), scale.reshape(b, 1, d), shift.reshape(b, 1, d), w16,
      bias.reshape(1, p))


def _head_rms(o, width):
    parts = []
    for s in range(0, o.shape[1], width):
        oh = o[:, s:s + width]
        ms = jnp.mean(oh * oh, axis=-1, keepdims=True)
        parts.append(oh * lax.rsqrt(ms + NORM_EPS))
    return jnp.concatenate(parts, axis=1)


def _mix_hgrn2(o_ref, z_ref, g_ref):
    return (_head_rms(o_ref[...], HG_DK) * g_ref[...]) * _silu(z_ref[...])


def _mix_hyena(yc_ref, u_ref, g1_ref, skip_ref):
    return g1_ref[...] * (yc_ref[...] + u_ref[...] * skip_ref[...])


def _mix_retention(o_ref, z_ref, g_ref):
    return (_head_rms(o_ref[...], RT_DV) * g_ref[...]) * _silu(z_ref[...])


def _mix_lru(y_ref, z_ref):
    return y_ref[...] * _silu(z_ref[...])


def _out_proj_kernel(*refs, mix, n_mix, final):
    mix_refs = refs[:n_mix]
    w_ref, x_ref, gate_ref = refs[n_mix:n_mix + 3]
    o_ref = refs[-1]
    y = mix(*mix_refs)
    out = x_ref[...] + gate_ref[...] * _bdot(y, w_ref[...])
    if final:
        fg_ref = refs[n_mix + 3]
        ms = jnp.mean(out * out, axis=-1, keepdims=True)
        out = out * lax.rsqrt(ms + NORM_EPS) * fg_ref[...]
    o_ref[...] = out


def _out_proj(mix, mix_args, mix_specs, w16, x, gate, final_g=None):
    b, l, d = x.shape
    e = w16.shape[0]
    tm = min(l, 256)
    in_specs = list(mix_specs(tm)) + [
        pl.BlockSpec((e, d), lambda bi, i: (0, 0)),
        pl.BlockSpec((None, tm, d), lambda bi, i: (bi, i, 0)),
        pl.BlockSpec((None, 1, d), lambda bi, i: (bi, 0, 0))]
    args = list(mix_args) + [w16, x, gate.reshape(b, 1, d)]
    if final_g is not None:
        in_specs.append(pl.BlockSpec((1, d), lambda bi, i: (0, 0)))
        args.append(final_g.reshape(1, d))
    return pl.pallas_call(
        functools.partial(_out_proj_kernel, mix=mix, n_mix=len(mix_args), final=final_g is not None),
        grid=(b, l // tm),
        in_specs=in_specs,
        out_specs=pl.BlockSpec((None, tm, d), lambda bi, i: (bi, i, 0)),
        out_shape=jax.ShapeDtypeStruct((b, l, d), F32),
        compiler_params=_cparams(("arbitrary", "arbitrary"), 48),
        name="out_proj",
    )(*args)


def _row_spec(tm, width, col):
    return pl.BlockSpec((None, tm, width), lambda bi, i: (bi, i, col))


def _vec_spec(width):
    return pl.BlockSpec((1, width), lambda bi, i: (0, 0))


def _hgrn2_kernel(*refs, reverse, hb, nch, add):
    q_ref, f_ref, v_ref, lb_ref = refs[:4]
    prev_ref = refs[4] if add else None
    o_ref, st_ref = refs[-2:]

    @pl.when(pl.program_id(2) == 0)
    def _():
        st_ref[...] = jnp.zeros_like(st_ref)

    lb_exp = jnp.exp(lb_ref[...] - jnp.max(lb_ref[...], axis=0, keepdims=True))
    lb_all = lb_exp[0:1, :] / jnp.sum(lb_exp, axis=0, keepdims=True)

    c = HG_CHUNK
    row = lax.broadcasted_iota(jnp.int32, (c, c), 0)
    col = lax.broadcasted_iota(jnp.int32, (c, c), 1)
    mask = (col >= row) if reverse else (col <= row)
    rowk = lax.broadcasted_iota(jnp.int32, (c, HG_DK), 0)
    mid = c // 2
    ref_row = (c - 1 - mid) if reverse else mid
    last_row = 0 if reverse else c - 1

    def chunk(ci, carry):
        cc = (nch - 1 - ci) if reverse else ci
        r0 = pl.multiple_of(cc * c, c)
        for hh in range(hb):
            sl = slice(hh * HG_DK, (hh + 1) * HG_DK)
            q = _silu(q_ref[pl.ds(r0, c), sl])
            lb = lb_all[:, sl]
            f = lb + (1.0 - lb) * _sigmoid(f_ref[pl.ds(r0, c), sl])
            k = 1.0 - f
            g = jnp.log(f)
            v = v_ref[pl.ds(r0, c), sl]
            bsum = g
            s = 1
            while s < c:
                shifted = pltpu.roll(bsum, (c - s) if reverse else s, 0)
                valid = (rowk < c - s) if reverse else (rowk >= s)
                bsum = bsum + jnp.where(valid, shifted, 0.0)
                s *= 2
            b_ref_row = bsum[ref_row:ref_row + 1, :]
            b_last = bsum[last_row:last_row + 1, :]
            scores = _bdot_nt(q * jnp.exp(bsum - b_ref_row), k * jnp.exp(b_ref_row - bsum))
            scores = jnp.where(mask, scores, 0.0)
            st = st_ref[hh]
            vt = v.T
            o = _bdot_nt(jnp.concatenate([q * jnp.exp(bsum), scores], axis=1),
                         jnp.concatenate([st, vt], axis=1))
            st_ref[hh] = st * jnp.exp(b_last) + _bdot(vt, k * jnp.exp(b_last - bsum))
            if add:
                o = o + prev_ref[pl.ds(r0, c), sl]
            o_ref[pl.ds(r0, c), sl] = o
        return carry

    lax.fori_loop(0, nch, chunk, 0, unroll=2)


def _hgrn2_dir(proj, lb, reverse, prev):
    b, l, _ = proj.shape
    e = E_WIDTH
    hb = 8
    w = hb * HG_DK
    t = min(l, 512)
    nt = l // t
    ncol = e // w
    fsec = 2 if reverse else 1

    def rows(bi, h, ti):
        return (nt - 1 - ti) if reverse else ti

    def sec(s):
        return pl.BlockSpec((None, t, w), lambda bi, h, ti: (bi, rows(bi, h, ti), s * ncol + h))

    in_specs = [sec(0), sec(fsec), sec(3), pl.BlockSpec((DEPTH + 1, w), lambda bi, h, ti: (0, h))]
    args = [proj, proj, proj, lb]
    out_spec = pl.BlockSpec((None, t, w), lambda bi, h, ti: (bi, rows(bi, h, ti), h))
    if prev is not None:
        in_specs.append(out_spec)
        args.append(prev)
    return pl.pallas_call(
        functools.partial(_hgrn2_kernel, reverse=reverse, hb=hb, nch=t // HG_CHUNK,
                          add=prev is not None),
        grid=(b, ncol, nt),
        in_specs=in_specs,
        out_specs=out_spec,
        out_shape=jax.ShapeDtypeStruct((b, l, e), F32),
        scratch_shapes=[pltpu.VMEM((hb, HG_DK, HG_DK), F32)],
        compiler_params=_cparams(("arbitrary", "arbitrary", "arbitrary"), 32),
        name="hgrn2_bwd" if reverse else "hgrn2_fwd",
    )(*args)


def _ret_tables(reverse):
    c = RT_CHUNK
    hidx = np.arange(RT_HEADS, dtype=np.float64)
    lg = np.log1p(-np.exp2((-5.5 if reverse else -5.0) - hidx))[:, None]
    pos = np.arange(c, dtype=np.float64)[None, :]
    rel = pos[0][:, None] - pos[0][None, :]
    if reverse:
        rel = -rel
    decay = np.where(rel >= 0, np.exp(lg[:, :, None] * np.maximum(rel, 0.0)[None]), 0.0)
    q_dec = np.exp(lg * ((c - pos) if reverse else (pos + 1.0)))
    k_dec = np.exp(lg * (pos if reverse else (c - 1.0 - pos)))
    c_dec = np.exp(lg * c)
    return (jnp.asarray(decay, F32),
            jnp.asarray(np.broadcast_to(q_dec[:, :, None], (RT_HEADS, c, RT_DV)), F32),
            jnp.asarray(np.broadcast_to(k_dec[:, :, None], (RT_HEADS, c, RT_DK)), F32),
            jnp.asarray(np.broadcast_to(c_dec[:, :, None], (RT_HEADS, 1, RT_DV)), F32))


def _rope_tables(l):
    inv = RT_ROPE_BASE ** (-jnp.arange(0, RT_DK, 2, dtype=F32) / RT_DK)
    ang = jnp.arange(l, dtype=F32)[:, None] * inv[None]
    return jnp.cos(ang), jnp.sin(ang)


def _ret_kernel(*refs, add):
    q_ref, k_ref, v_ref, cos_ref, sin_ref, dec_ref, qd_ref, kd_ref, cd_ref = refs[:9]
    prev_ref = refs[9] if add else None
    o_ref, r_ref = refs[-2:]

    @pl.when(pl.program_id(2) == 0)
    def _():
        r_ref[...] = jnp.zeros_like(r_ref)

    cos = cos_ref[...]
    sin = sin_ref[...]
    half = RT_DK // 2

    def rot(t):
        t1 = t[:, :half]
        t2 = t[:, half:]
        return jnp.concatenate([t1 * cos - t2 * sin, t1 * sin + t2 * cos], axis=1)

    q = rot(q_ref[...])
    k = rot(k_ref[...]) * (RT_DK ** -0.5)
    v = v_ref[...]
    scores = _bdot_nt(q, k) * dec_ref[...]
    r = r_ref[...]
    o = _bdot(scores, v) + qd_ref[...] * _bdot(q, r)
    r_ref[...] = cd_ref[...] * r + _bdot((k * kd_ref[...]).T, v)
    if add:
        o = o + prev_ref[...]
    o_ref[...] = o


def _ret_dir(proj, cos, sin, reverse, prev):
    b, l, _ = proj.shape
    c = RT_CHUNK
    nt = l // c
    dec, qd, kd, cd = _ret_tables(reverse)

    def rows(ti):
        return (nt - 1 - ti) if reverse else ti

    in_specs = [
        pl.BlockSpec((None, c, RT_DK), lambda bi, h, ti: (bi, rows(ti), h)),
        pl.BlockSpec((None, c, RT_DK), lambda bi, h, ti: (bi, rows(ti), RT_HEADS + h)),
        pl.BlockSpec((None, c, RT_DV), lambda bi, h, ti: (bi, rows(ti), 2 * RT_QK // RT_DV + h)),
        pl.BlockSpec((c, RT_DK // 2), lambda bi, h, ti: (rows(ti), 0)),
        pl.BlockSpec((c, RT_DK // 2), lambda bi, h, ti: (rows(ti), 0)),
        pl.BlockSpec((None, c, c), lambda bi, h, ti: (h, 0, 0)),
        pl.BlockSpec((None, c, RT_DV), lambda bi, h, ti: (h, 0, 0)),
        pl.BlockSpec((None, c, RT_DK), lambda bi, h, ti: (h, 0, 0)),
        pl.BlockSpec((None, 1, RT_DV), lambda bi, h, ti: (h, 0, 0)),
    ]
    args = [proj, proj, proj, cos, sin, dec, qd, kd, cd]
    out_spec = pl.BlockSpec((None, c, RT_DV), lambda bi, h, ti: (bi, rows(ti), h))
    if prev is not None:
        in_specs.append(out_spec)
        args.append(prev)
    return pl.pallas_call(
        functools.partial(_ret_kernel, add=prev is not None),
        grid=(b, RT_HEADS, nt),
        in_specs=in_specs,
        out_specs=out_spec,
        out_shape=jax.ShapeDtypeStruct((b, l, E_WIDTH), F32),
        scratch_shapes=[pltpu.VMEM((RT_DK, RT_DV), F32)],
        compiler_params=_cparams(("arbitrary", "arbitrary", "arbitrary"), 32),
        name="ret_bwd" if reverse else "ret_fwd",
    )(*args)


def _halo_specs(t, w, l, col, order):
    per = t // SUBLANES
    nblk = l // SUBLANES
    prev = pl.BlockSpec((None, SUBLANES, w),
                        lambda *g: (g[0], jnp.maximum(order(*g) * per - 1, 0), col(*g)))
    nxt = pl.BlockSpec((None, SUBLANES, w),
                       lambda *g: (g[0], jnp.minimum((order(*g) + 1) * per, nblk - 1), col(*g)))
    return prev, nxt


def _fill_ext(ext_ref, x_ref, xp_ref, xn_ref, first, last, t):
    ext_ref[0:SUBLANES, :] = jnp.where(first, 0.0, xp_ref[...])
    ext_ref[SUBLANES:SUBLANES + t, :] = x_ref[...]
    ext_ref[SUBLANES + t:2 * SUBLANES + t, :] = jnp.where(last, 0.0, xn_ref[...])


def _lru_kernel(*refs, reverse, t, nt, add):
    x_ref, xp_ref, xn_ref, cw_ref, cb_ref, gw_ref, gb_ref, lam_ref = refs[:8]
    prev_ref = refs[8] if add else None
    o_ref, ext_ref, carry_ref = refs[-3:]
    ti = pl.program_id(1)
    te = (nt - 1 - ti) if reverse else ti

    @pl.when(ti == 0)
    def _():
        carry_ref[...] = jnp.zeros_like(carry_ref)

    _fill_ext(ext_ref, x_ref, xp_ref, xn_ref, te == 0, te == nt - 1, t)
    left = LRU_CONV // 2
    xb = cb_ref[...]
    for j in range(LRU_CONV):
        xb = xb + cw_ref[j:j + 1, :] * ext_ref[pl.ds(SUBLANES - left + j, t), :]

    neg_lam = -lam_ref[...]
    softplus = jnp.maximum(neg_lam, 0.0) + jnp.log1p(jnp.exp(-jnp.abs(neg_lam)))
    rowi = lax.broadcasted_iota(jnp.int32, (t, LRU_BS), 0)

    for n in range(LRU_BLOCKS):
        sl = slice(n * LRU_BS, (n + 1) * LRU_BS)
        xn = xb[:, sl]
        gates = _bdot(xn, gw_ref[n]) + gb_ref[n]
        r = _sigmoid(gates[:, :LRU_BS])
        i = _sigmoid(gates[:, LRU_BS:])
        log_a = -LRU_C * r * softplus[:, sl]
        a = jnp.exp(log_a)
        bb = jnp.sqrt(-_expm1(2.0 * log_a)) * (i * xn)
        s = 1
        while s < t:
            shift = (t - s) if reverse else s
            valid = (rowi < t - s) if reverse else (rowi >= s)
            a_sh = pltpu.roll(a, shift, 0)
            b_sh = pltpu.roll(bb, shift, 0)
            bb = jnp.where(valid, a * b_sh + bb, bb)
            a = jnp.where(valid, a * a_sh, a)
            s *= 2
        h = bb + a * carry_ref[:, sl]
        edge = 0 if reverse else t - 1
        carry_ref[:, sl] = h[edge:edge + 1, :]
        if add:
            h = h + prev_ref[:, sl]
        o_ref[:, sl] = h


def _lru_dir(proj, conv_w, conv_b, gate_w16, gate_b, lam, reverse, prev):
    b, l, _ = proj.shape
    e = E_WIDTH
    t = min(l, 256)
    nt = l // t

    def order(bi, ti):
        return (nt - 1 - ti) if reverse else ti

    xp_spec, xn_spec = _halo_specs(t, e, l, lambda bi, ti: 0, order)
    in_specs = [
        pl.BlockSpec((None, t, e), lambda bi, ti: (bi, order(bi, ti), 0)), xp_spec, xn_spec,
        pl.BlockSpec((LRU_CONV, e), lambda bi, ti: (0, 0)),
        pl.BlockSpec((1, e), lambda bi, ti: (0, 0)),
        pl.BlockSpec((LRU_BLOCKS, LRU_BS, 2 * LRU_BS), lambda bi, ti: (0, 0, 0)),
        pl.BlockSpec((LRU_BLOCKS, 1, 2 * LRU_BS), lambda bi, ti: (0, 0, 0)),
        pl.BlockSpec((1, e), lambda bi, ti: (0, 0)),
    ]
    args = [proj, proj, proj, conv_w, conv_b.reshape(1, e), gate_w16, gate_b, lam.reshape(1, e)]
    out_spec = pl.BlockSpec((None, t, e), lambda bi, ti: (bi, order(bi, ti), 0))
    if prev is not None:
        in_specs.append(out_spec)
        args.append(prev)
    return pl.pallas_call(
        functools.partial(_lru_kernel, reverse=reverse, t=t, nt=nt, add=prev is not None),
        grid=(b, nt),
        in_specs=in_specs,
        out_specs=out_spec,
        out_shape=jax.ShapeDtypeStruct((b, l, e), F32),
        scratch_shapes=[pltpu.VMEM((t + 2 * SUBLANES, e), F32), pltpu.VMEM((1, e), F32)],
        compiler_params=_cparams(("arbitrary", "arbitrary"), 48),
        name="lru_bwd" if reverse else "lru_fwd",
    )(*args)


def _hy_pre_kernel(x0_ref, x0p_ref, x0n_ref, x1_ref, x1p_ref, x1n_ref, v_ref, vp_ref, vn_ref,
                   z_ref, w0_ref, w1_ref, wv_ref, b0_ref, b1_ref, bv_ref,
                   u_ref, g1_ref, e0_ref, e1_ref, ev_ref, *, t, nt):
    ti = pl.program_id(1)
    first = ti == 0
    last = ti == nt - 1

    def conv(x_ref, xp_ref, xn_ref, ext_ref, w_ref, b_ref):
        _fill_ext(ext_ref, x_ref, xp_ref, xn_ref, first, last, t)
        out = b_ref[...]
        for j in range(3):
            out = out + w_ref[j:j + 1, :] * ext_ref[pl.ds(SUBLANES - 1 + j, t), :]
        return out

    x0 = conv(x0_ref, x0p_ref, x0n_ref, e0_ref, w0_ref, b0_ref)
    x1 = conv(x1_ref, x1p_ref, x1n_ref, e1_ref, w1_ref, b1_ref)
    v = conv(v_ref, vp_ref, vn_ref, ev_ref, wv_ref, bv_ref)
    u_ref[...] = x0 * v
    g1_ref[...] = x1 * _silu(z_ref[...])


def _hy_pre(proj, conv_w, conv_b):
    b, l, _ = proj.shape
    e = E_WIDTH
    w = 512
    t = min(l, 512)
    nt = l // t
    ncol = e // w

    def order(bi, ti, j):
        return ti

    specs, args = [], []
    for s in range(3):
        col = (lambda s: lambda bi, ti, j: s * ncol + j)(s)
        xp, xn = _halo_specs(t, w, l, col, order)
        specs += [pl.BlockSpec((None, t, w), (lambda col: lambda bi, ti, j: (bi, ti, col(bi, ti, j)))(col)),
                  xp, xn]
        args += [proj, proj, proj]
    specs.append(pl.BlockSpec((None, t, w), lambda bi, ti, j: (bi, ti, 3 * ncol + j)))
    args.append(proj)
    for s in range(3):
        specs.append(pl.BlockSpec((3, w), (lambda s: lambda bi, ti, j: (0, s * ncol + j))(s)))
        args.append(conv_w)
    cb = conv_b.reshape(1, 3 * e)
    for s in range(3):
        specs.append(pl.BlockSpec((1, w), (lambda s: lambda bi, ti, j: (0, s * ncol + j))(s)))
        args.append(cb)
    out_spec = pl.BlockSpec((None, t, w), lambda bi, ti, j: (bi, ti, j))
    return pl.pallas_call(
        functools.partial(_hy_pre_kernel, t=t, nt=nt),
        grid=(b, nt, ncol),
        in_specs=specs,
        out_specs=[out_spec, out_spec],
        out_shape=[jax.ShapeDtypeStruct((b, l, e), F32)] * 2,
        scratch_shapes=[pltpu.VMEM((t + 2 * SUBLANES, w), F32)] * 3,
        compiler_params=_cparams(("arbitrary", "arbitrary", "arbitrary"), 48),
        name="hy_pre",
    )(*args)


def _hy_filter_kernel(z_ref, w1_ref, b1_ref, w2_ref, b2_ref, fr_ref, wf_ref, wb_ref, dl_ref,
                      sd_ref, a_ref):
    @pl.when(pl.program_id(1) == 0)
    def _():
        fr = fr_ref[...]
        a = jnp.sin(fr * (_bdot(z_ref[...], w1_ref[...]) + b1_ref[...]))
        for j in range(HY_INNER):
            a = jnp.sin(fr * (_bdot(a, w2_ref[j]) + b2_ref[j]))
        a_ref[...] = a

    a = a_ref[...]
    window = jnp.exp(-z_ref[:, 0:1] * dl_ref[...])
    h_fw = _bdot(a, wf_ref[...]) * window
    h_bw = _bdot(a, wb_ref[...]) * window
    sd_ref[0] = h_fw + h_bw
    sd_ref[1] = h_fw - h_bw


def _hy_filters(l, w1, b1, w2, b2, wout16, freq):
    e = E_WIDTH
    kp = LANES
    t = jnp.linspace(0.0, 1.0, l, dtype=F32)[:, None]
    wv = 2.0 * math.pi * jnp.arange(l, dtype=F32)[:, None] / l
    bands = jnp.linspace(1e-4, HY_BANDS - 1, HY_BANDS, dtype=F32)[None]
    z = jnp.concatenate([t, jnp.cos(bands * wv), -jnp.sin(bands * wv),
                         jnp.zeros((l, kp - HY_EMB), F32)], axis=-1)
    w1p = jnp.concatenate([w1, jnp.zeros((kp - HY_EMB, HY_FH), F32)], axis=0)
    max_decay = math.log(HY_TARGET) / HY_FAST_DECAY
    min_decay = math.log(HY_TARGET) / HY_SLOW_DECAY
    deltas = jnp.abs(jnp.linspace(min_decay, max_decay, e, dtype=F32))[None]
    tm = min(l, 512)
    w = 512
    ncol = e // w
    return pl.pallas_call(
        _hy_filter_kernel,
        grid=(l // tm, ncol),
        in_specs=[pl.BlockSpec((tm, kp), lambda i, j: (i, 0)),
                  pl.BlockSpec((kp, HY_FH), lambda i, j: (0, 0)),
                  pl.BlockSpec((1, HY_FH), lambda i, j: (0, 0)),
                  pl.BlockSpec((HY_INNER, HY_FH, HY_FH), lambda i, j: (0, 0, 0)),
                  pl.BlockSpec((HY_INNER, 1, HY_FH), lambda i, j: (0, 0, 0)),
                  pl.BlockSpec((1, HY_FH), lambda i, j: (0, 0)),
                  pl.BlockSpec((HY_FH, w), lambda i, j: (0, j)),
                  pl.BlockSpec((HY_FH, w), lambda i, j: (0, ncol + j)),
                  pl.BlockSpec((1, w), lambda i, j: (0, j))],
        out_specs=pl.BlockSpec((2, tm, w), lambda i, j: (0, i, j)),
        out_shape=jax.ShapeDtypeStruct((2, l, e), F32),
        scratch_shapes=[pltpu.VMEM((tm, HY_FH), F32)],
        compiler_params=_cparams(("arbitrary", "arbitrary"), 32),
        name="hy_filter",
    )(z, w1p, b1.reshape(1, HY_FH), w2, b2.reshape(HY_INNER, 1, HY_FH), freq.reshape(1, HY_FH),
      wout16, wout16, deltas)


FFT_UNROLL = 16


def _fft_dims(l):
    n = 2 * l
    n1 = int(round(math.sqrt(n)))
    assert n1 * n1 == n and n1 % 16 == 0, "sequence length must give a square DFT factorisation"
    return n1, n1


def _fft_pitch(n1):
    return 2 * n1 + SUBLANES


def _fft_tables(l):
    n1, n2 = _fft_dims(l)
    n = n1 * n2
    k1 = np.arange(n1)[:, None]
    m1 = np.arange(n1 // 2)[None, :]
    j2 = np.arange(n2)[:, None, None]
    ang = -2.0 * np.pi * (k1 * m1 / n1)[None] - 2.0 * np.pi * (j2 * k1[None] / n)
    g_fwd = np.concatenate([np.cos(ang), np.sin(ang)], axis=1)
    ang_i = 2.0 * np.pi * (m1.T * k1.T / n1)[None] + 2.0 * np.pi * (j2 * k1.T[None] / n)
    g_inv = np.concatenate([np.cos(ang_i), -np.sin(ang_i)], axis=2) / n
    a2 = -2.0 * np.pi * np.arange(n2)[:, None] * np.arange(n2)[None, :] / n2
    fr, fi = np.cos(a2), np.sin(a2)
    f2 = np.block([[fr, -fi], [fi, fr]])
    f2_inv = np.block([[fr, fi], [-fi, fr]])
    f2_half = np.stack([np.concatenate([fr, -fi], axis=1), np.concatenate([fi, fr], axis=1)])
    as16 = lambda a: jnp.asarray(a, F32).astype(BF16)
    return as16(g_fwd), as16(g_inv), as16(f2), as16(f2_inv), as16(f2_half)


def _fft_stage1(x_ref, g_ref, work_ref, n2_lo, cnt, n1, n2, pitch):
    def body(j, carry):
        jj = n2_lo + j
        xs = x_ref[pl.ds(jj, n1 // 2, stride=n2), :]
        r0 = pl.multiple_of(jj * pitch, SUBLANES)
        work_ref[pl.ds(r0, 2 * n1), :] = _bdot(g_ref[j], xs)
        return carry
    lax.fori_loop(0, cnt, body, 0, unroll=FFT_UNROLL)


def _fft_load_k1(work_ref, k1, n1, n2, pitch):
    br = work_ref[pl.ds(k1, n2, stride=pitch), :]
    bi = work_ref[pl.ds(n1 + k1, n2, stride=pitch), :]
    return jnp.concatenate([br, bi], axis=0)


def _hy_spec_kernel(x_ref, g_ref, f2_ref, t_ref, work_ref, *, n1, n2, nc):
    p = pl.program_id(2)
    pitch = _fft_pitch(n1)
    c2 = n2 // nc
    c1 = n1 // nc

    @pl.when(p < nc)
    def _():
        _fft_stage1(x_ref, g_ref, work_ref, p * c2, c2, n1, n2, pitch)

    @pl.when(p >= nc)
    def _():
        def body(j, carry):
            k1 = (p - nc) * c1 + j
            t_ref[j] = _bdot(f2_ref[...], _fft_load_k1(work_ref, k1, n1, n2, pitch))
            return carry
        lax.fori_loop(0, c1, body, 0, unroll=FFT_UNROLL)


def _fft_nc(l):
    return 8 if l >= 8192 else (2 if l >= 2048 else 1)


def _hy_spectrum(sd, tables):
    _, l, e = sd.shape
    n1, n2 = _fft_dims(l)
    nc = _fft_nc(l)
    g_fwd, _, _, _, f2_half = tables
    pitch = _fft_pitch(n1)
    return pl.pallas_call(
        functools.partial(_hy_spec_kernel, n1=n1, n2=n2, nc=nc),
        grid=(e // LANES, 2, 2 * nc),
        in_specs=[pl.BlockSpec((None, l, LANES), lambda c, j, p: (j, 0, c)),
                  pl.BlockSpec((n2 // nc, 2 * n1, n1 // 2), lambda c, j, p: (jnp.minimum(p, nc - 1), 0, 0)),
                  pl.BlockSpec((None, n2, 2 * n2), lambda c, j, p: (j, 0, 0))],
        out_specs=pl.BlockSpec((n1 // nc, n2, LANES),
                               lambda c, j, p: (jnp.maximum(p - nc, 0), j, c)),
        out_shape=jax.ShapeDtypeStruct((n1, 2 * n2, e), F32),
        scratch_shapes=[pltpu.VMEM((n2 * pitch, LANES), F32)],
        compiler_params=_cparams(("arbitrary", "arbitrary", "arbitrary"), 48),
        name="hy_spectrum",
    )(sd, g_fwd, f2_half)


def _hy_conv_kernel(u_ref, gf_ref, f2_ref, f2i_ref, t_ref, gi_ref, y_ref, work_ref, *, n1, n2, nc):
    p = pl.program_id(2)
    pitch = _fft_pitch(n1)
    c2 = n2 // nc
    c1 = n1 // nc

    @pl.when(p < nc)
    def _():
        _fft_stage1(u_ref, gf_ref, work_ref, p * c2, c2, n1, n2, pitch)

    @pl.when(jnp.logical_and(p >= nc, p < 2 * nc))
    def _():
        def body(j, carry):
            k1 = (p - nc) * c1 + j
            x = _bdot(f2_ref[...], _fft_load_k1(work_ref, k1, n1, n2, pitch))
            xr, xi = x[:n2], x[n2:]
            tr, ti = t_ref[j, :n2, :], t_ref[j, n2:, :]
            z = jnp.concatenate([xr * tr - xi * ti, xr * ti + xi * tr], axis=0)
            cmat = _bdot(f2i_ref[...], z)
            work_ref[pl.ds(k1, n2, stride=pitch), :] = cmat[:n2]
            work_ref[pl.ds(n1 + k1, n2, stride=pitch), :] = cmat[n2:]
            return carry
        lax.fori_loop(0, c1, body, 0, unroll=FFT_UNROLL)

    @pl.when(p >= 2 * nc)
    def _():
        def body(j, carry):
            jj = (p - 2 * nc) * c2 + j
            r0 = pl.multiple_of(jj * pitch, SUBLANES)
            d = work_ref[pl.ds(r0, 2 * n1), :]
            y_ref[pl.ds(jj, n1 // 2, stride=n2), :] = _bdot(gi_ref[j], d)
            return carry
        lax.fori_loop(0, c2, body, 0, unroll=FFT_UNROLL)


def _hy_conv(u, spec, tables):
    b, l, e = u.shape
    n1, n2 = _fft_dims(l)
    nc = _fft_nc(l)
    g_fwd, g_inv, f2, f2_inv, _ = tables
    pitch = _fft_pitch(n1)
    clip = lambda v: jnp.clip(v, 0, nc - 1)
    return pl.pallas_call(
        functools.partial(_hy_conv_kernel, n1=n1, n2=n2, nc=nc),
        grid=(e // LANES, b, 3 * nc),
        in_specs=[pl.BlockSpec((None, l, LANES), lambda c, bi, p: (bi, 0, c)),
                  pl.BlockSpec((n2 // nc, 2 * n1, n1 // 2), lambda c, bi, p: (clip(p), 0, 0)),
                  pl.BlockSpec((2 * n2, 2 * n2), lambda c, bi, p: (0, 0)),
                  pl.BlockSpec((2 * n2, 2 * n2), lambda c, bi, p: (0, 0)),
                  pl.BlockSpec((n1 // nc, 2 * n2, LANES), lambda c, bi, p: (clip(p - nc), 0, c)),
                  pl.BlockSpec((n2 // nc, n1 // 2, 2 * n1), lambda c, bi, p: (clip(p - 2 * nc), 0, 0))],
        out_specs=pl.BlockSpec((None, l, LANES), lambda c, bi, p: (bi, 0, c)),
        out_shape=jax.ShapeDtypeStruct((b, l, e), F32),
        scratch_shapes=[pltpu.VMEM((n2 * pitch, LANES), F32)],
        compiler_params=_cparams(("arbitrary", "arbitrary", "arbitrary"), 52),
        name="hy_conv",
    )(u, g_fwd, f2, f2_inv, spec, g_inv)


def _prep_weights(p):
    c16 = lambda a: a.astype(BF16)
    gw = p['lru_gate_w'][0]
    gw = jnp.concatenate([gw[:, 0], gw[:, 1]], axis=-1)
    gb = p['lru_gate_b'][0].reshape(2, 2, LRU_BLOCKS, 1, LRU_BS)
    gb = jnp.concatenate([gb[:, 0], gb[:, 1]], axis=-1)
    lb = p['hg_lb'].astype(F32)
    return dict(
        ada_w=c16(p['ada_w']), hg_w_in=c16(p['hg_w_in'][0]), hg_w_out=c16(p['hg_w_out'][0]),
        hy_w_in=c16(p['hy_w_in'][0]), hy_w_out=c16(p['hy_w_out'][0]), hy_f_wout=c16(p['hy_f_wout'][0]),
        rt_w_in=c16(p['rt_w_in'][0]), rt_w_out=c16(p['rt_w_out'][0]),
        lru_w_in=c16(p['lru_w_in'][0]), lru_w_out=c16(p['lru_w_out'][0]),
        lru_gate_w=c16(gw), lru_gate_b=gb, hg_lb=lb,
        hg_norm_g=jnp.tile(p['hg_norm_g'][0], HG_HEADS))


def _trunk(x, mod, p, w):
    b, l, d = x.shape
    e = E_WIDTH
    x = x.astype(F32)
    zero_bias = lambda n: jnp.zeros((n,), F32)

    def split(layer):
        m = mod[layer]
        return m[:, :d], m[:, d:2 * d], m[:, 2 * d:]

    shift, scale, gate = split(0)
    proj = _in_proj(x, p['norm_g'][0], scale, shift, w['hg_w_in'], zero_bias(5 * e))
    o = _hgrn2_dir(proj, w['hg_lb'], False, None)
    o = _hgrn2_dir(proj, w['hg_lb'], True, o)
    x = _out_proj(_mix_hgrn2, [o, proj, w['hg_norm_g'].reshape(1, e)],
                  lambda tm: [_row_spec(tm, e, 0), _row_spec(tm, e, 4), _vec_spec(e)],
                  w['hg_w_out'], x, gate)

    shift, scale, gate = split(1)
    proj = _in_proj(x, p['norm_g'][1], scale, shift, w['hy_w_in'], p['hy_b_in'][0])
    u, g1 = _hy_pre(proj, p['hy_conv_w'][0], p['hy_conv_b'][0])
    tables = _fft_tables(l)
    sd = _hy_filters(l, p['hy_f_w1'][0], p['hy_f_b1'][0], p['hy_f_w2'][0], p['hy_f_b2'][0],
                     w['hy_f_wout'], p['hy_f_freq'][0])
    spec = _hy_spectrum(sd, tables)
    yc = _hy_conv(u, spec, tables)
    x = _out_proj(_mix_hyena, [yc, u, g1, p['hy_skip'][0].reshape(1, e)],
                  lambda tm: [_row_spec(tm, e, 0)] * 3 + [_vec_spec(e)],
                  w['hy_w_out'], x, gate)

    shift, scale, gate = split(2)
    proj = _in_proj(x, p['norm_g'][2], scale, shift, w['rt_w_in'], zero_bias(2 * RT_QK + 2 * e))
    cos, sin = _rope_tables(l)
    o = _ret_dir(proj, cos, sin, False, None)
    o = _ret_dir(proj, cos, sin, True, o)
    x = _out_proj(_mix_retention, [o, proj, p['rt_gn_g'][0].reshape(1, e)],
                  lambda tm: [_row_spec(tm, e, 0), _row_spec(tm, e, 2), _vec_spec(e)],
                  w['rt_w_out'], x, gate)

    shift, scale, gate = split(3)
    proj = _in_proj(x, p['norm_g'][3], scale, shift, w['lru_w_in'], zero_bias(2 * e))
    y = None
    for dirn in range(2):
        y = _lru_dir(proj, p['lru_conv_w'][0], p['lru_conv_b'][0], w['lru_gate_w'][dirn],
                     w['lru_gate_b'][dirn], p['lru_lambda'][0][dirn], dirn == 1, y)
    return _out_proj(_mix_lru, [y, proj], lambda tm: [_row_spec(tm, e, 0), _row_spec(tm, e, 1)],
                     w['lru_w_out'], x, gate, final_g=p['final_g'])


def kernel(x_prompt, x_sample, c_prompt, c_sample, ada_w, ada_b, norm_g, final_g, hg_lb, hg_w_in, hg_norm_g, hg_w_out, hy_w_in, hy_b_in, hy_conv_w, hy_conv_b, hy_f_w1, hy_f_b1, hy_f_w2, hy_f_b2, hy_f_wout, hy_f_freq, hy_skip, hy_w_out, rt_w_in, rt_gn_g, rt_w_out, lru_w_in, lru_conv_w, lru_conv_b, lru_gate_w, lru_gate_b, lru_lambda, lru_w_out):
    p = dict(ada_w=ada_w, ada_b=ada_b, norm_g=norm_g, final_g=final_g, hg_lb=hg_lb, hg_w_in=hg_w_in,
             hg_norm_g=hg_norm_g, hg_w_out=hg_w_out, hy_w_in=hy_w_in, hy_b_in=hy_b_in,
             hy_conv_w=hy_conv_w, hy_conv_b=hy_conv_b, hy_f_w1=hy_f_w1, hy_f_b1=hy_f_b1,
             hy_f_w2=hy_f_w2, hy_f_b2=hy_f_b2, hy_f_wout=hy_f_wout, hy_f_freq=hy_f_freq,
             hy_skip=hy_skip, hy_w_out=hy_w_out, rt_w_in=rt_w_in, rt_gn_g=rt_gn_g, rt_w_out=rt_w_out,
             lru_w_in=lru_w_in, lru_conv_w=lru_conv_w, lru_conv_b=lru_conv_b, lru_gate_w=lru_gate_w,
             lru_gate_b=lru_gate_b, lru_lambda=lru_lambda, lru_w_out=lru_w_out)
    w = _prep_weights(p)
    bp, bs = c_prompt.shape[0], c_sample.shape[0]
    rows = -(-(bp + bs) // SUBLANES) * SUBLANES
    c_all = jnp.concatenate([c_prompt, c_sample, jnp.zeros((rows - bp - bs, D_MODEL), F32)], axis=0)
    mod = _adaln(c_all.astype(F32), w['ada_w'], ada_b)
    y_prompt = _trunk(x_prompt, mod[:, :bp], p, w).astype(x_prompt.dtype)
    y_sample = _trunk(x_sample, mod[:, bp:bp + bs], p, w).astype(x_sample.dtype)
    return (y_prompt, y_sample)
```

```python
import functools
import math

import numpy as np
import jax
import jax.numpy as jnp
from jax import lax
from jax.experimental import pallas as pl
from jax.experimental.pallas import tpu as pltpu

F32 = jnp.float32
BF16 = jnp.bfloat16

D_MODEL = 1024
DEPTH = 4
E_WIDTH = 2 * D_MODEL
NORM_EPS = 1e-6
LANES = 128
SUBLANES = 8
MIB = 1024 * 1024

HG_CHUNK = 64
HG_DK = 128
HG_HEADS = E_WIDTH // HG_DK

HY_EMB = 33
HY_BANDS = 16
HY_FH = 64
HY_INNER = 2
HY_FAST_DECAY = 0.3
HY_SLOW_DECAY = 1.5
HY_TARGET = 1e-2

RT_HEADS = 4
RT_QK = D_MODEL
RT_DK = RT_QK // RT_HEADS
RT_DV = E_WIDTH // RT_HEADS
RT_ROPE_BASE = 10000.0
RT_CHUNK = 256

LRU_CONV = 4
LRU_BLOCKS = 16
LRU_BS = E_WIDTH // LRU_BLOCKS
LRU_C = 8.0

_NT = (((1,), (1,)), ((), ()))


def _cparams(sem, vmem_mib):
    return pltpu.CompilerParams(dimension_semantics=sem, vmem_limit_bytes=vmem_mib * MIB)


def _bdot(a, b):
    return jnp.dot(a.astype(BF16), b.astype(BF16), preferred_element_type=F32)


def _bdot_nt(a, b):
    return lax.dot_general(a.astype(BF16), b.astype(BF16), _NT, preferred_element_type=F32)


def _sigmoid(x):
    return jax.nn.sigmoid(x)


def _silu(x):
    return x * _sigmoid(x)


def _expm1(x):
    u = jnp.exp(x)
    plain = jnp.logical_or(u == 1.0, x < -0.5)
    small = (u - 1.0) * x / jnp.where(plain, 1.0, jnp.log(u))
    return jnp.where(u == 1.0, x, jnp.where(x < -0.5, u - 1.0, small))


def _adaln_kernel(c_ref, w_ref, b_ref, o_ref):
    cs = _silu(c_ref[...])
    o_ref[...] = _bdot(cs, w_ref[...]) + b_ref[...]


def _adaln(c_all, ada_w16, ada_b):
    bp, d = c_all.shape
    tn = 1024
    return pl.pallas_call(
        _adaln_kernel,
        grid=(DEPTH, 3 * d // tn),
        in_specs=[pl.BlockSpec((bp, d), lambda l, j: (0, 0)),
                  pl.BlockSpec((None, d, tn), lambda l, j: (l, 0, j)),
                  pl.BlockSpec((None, 1, tn), lambda l, j: (l, 0, j))],
        out_specs=pl.BlockSpec((None, bp, tn), lambda l, j: (l, 0, j)),
        out_shape=jax.ShapeDtypeStruct((DEPTH, bp, 3 * d), F32),
        compiler_params=_cparams(("arbitrary", "arbitrary"), 32),
        name="adaln",
    )(c_all, ada_w16, ada_b.reshape(DEPTH, 1, 3 * d))


def _in_proj_kernel(x_ref, g_ref, sc_ref, sh_ref, w_ref, b_ref, o_ref, h_ref, *, tm):
    r0 = pl.multiple_of(pl.program_id(2) * tm, tm)

    @pl.when(pl.program_id(1) == 0)
    def _():
        x = x_ref[...]
        ms = jnp.mean(x * x, axis=-1, keepdims=True)
        h = x * lax.rsqrt(ms + NORM_EPS) * g_ref[...] * (1.0 + sc_ref[...]) + sh_ref[...]
        h_ref[pl.ds(r0, tm), :] = h.astype(BF16)

    o_ref[...] = (jnp.dot(h_ref[pl.ds(r0, tm), :], w_ref[...], preferred_element_type=F32)
                  + b_ref[...])


IN_PROJ_TM = 1024
IN_PROJ_TN = 1024


def _in_proj(x, norm_g, scale, shift, w16, bias):
    b, l, d = x.shape
    p = w16.shape[1]
    tm = min(l, IN_PROJ_TM)
    tn = IN_PROJ_TN
    ni = l // tm
    x_rows = lambda bi, j, i: (bi, jnp.where(j == 0, i, ni - 1), 0)
    return pl.pallas_call(
        functools.partial(_in_proj_kernel, tm=tm),
        grid=(b, p // tn, ni),
        in_specs=[pl.BlockSpec((None, tm, d), x_rows),
                  pl.BlockSpec((1, d), lambda bi, j, i: (0, 0)),
                  pl.BlockSpec((None, 1, d), lambda bi, j, i: (bi, 0, 0)),
                  pl.BlockSpec((None, 1, d), lambda bi, j, i: (bi, 0, 0)),
                  pl.BlockSpec((d, tn), lambda bi, j, i: (0, j)),
                  pl.BlockSpec((1, tn), lambda bi, j, i: (0, j))],
        out_specs=pl.BlockSpec((None, tm, tn), lambda bi, j, i: (bi, i, j)),
        out_shape=jax.ShapeDtypeStruct((b, l, p), F32),
        scratch_shapes=[pltpu.VMEM((l, d), BF16)],
        compiler_params=_cparams(("arbitrary", "arbitrary", "arbitrary"), 48),
        name="in_proj",
    )(x, norm_g.reshape(1, d), scale.reshape(b, 1, d), shift.reshape(b, 1, d), w16,
      bias.reshape(1, p))


def _head_rms(o, width):
    parts = []
    for s in range(0, o.shape[1], width):
        oh = o[:, s:s + width]
        ms = jnp.mean(oh * oh, axis=-1, keepdims=True)
        parts.append(oh * lax.rsqrt(ms + NORM_EPS))
    return jnp.concatenate(parts, axis=1)


def _mix_hgrn2(o_ref, z_ref, g_ref):
    return (_head_rms(o_ref[...], HG_DK) * g_ref[...]) * _silu(z_ref[...])


def _mix_hyena(yc_ref, u_ref, g1_ref, skip_ref):
    return g1_ref[...] * (yc_ref[...] + u_ref[...] * skip_ref[...])


def _mix_retention(o_ref, z_ref, g_ref):
    return (_head_rms(o_ref[...], RT_DV) * g_ref[...]) * _silu(z_ref[...])


def _mix_lru(y_ref, z_ref):
    return y_ref[...] * _silu(z_ref[...])


def _out_proj_kernel(*refs, mix, n_mix, final):
    mix_refs = refs[:n_mix]
    w_ref, x_ref, gate_ref = refs[n_mix:n_mix + 3]
    o_ref = refs[-1]
    y = mix(*mix_refs)
    out = x_ref[...] + gate_ref[...] * _bdot(y, w_ref[...])
    if final:
        fg_ref = refs[n_mix + 3]
        ms = jnp.mean(out * out, axis=-1, keepdims=True)
        out = out * lax.rsqrt(ms + NORM_EPS) * fg_ref[...]
    o_ref[...] = out


def _out_proj(mix, mix_args, mix_specs, w16, x, gate, final_g=None):
    b, l, d = x.shape
    e = w16.shape[0]
    tm = min(l, 256)
    in_specs = list(mix_specs(tm)) + [
        pl.BlockSpec((e, d), lambda bi, i: (0, 0)),
        pl.BlockSpec((None, tm, d), lambda bi, i: (bi, i, 0)),
        pl.BlockSpec((None, 1, d), lambda bi, i: (bi, 0, 0))]
    args = list(mix_args) + [w16, x, gate.reshape(b, 1, d)]
    if final_g is not None:
        in_specs.append(pl.BlockSpec((1, d), lambda bi, i: (0, 0)))
        args.append(final_g.reshape(1, d))
    return pl.pallas_call(
        functools.partial(_out_proj_kernel, mix=mix, n_mix=len(mix_args), final=final_g is not None),
        grid=(b, l // tm),
        in_specs=in_specs,
        out_specs=pl.BlockSpec((None, tm, d), lambda bi, i: (bi, i, 0)),
        out_shape=jax.ShapeDtypeStruct((b, l, d), F32),
        compiler_params=_cparams(("arbitrary", "arbitrary"), 48),
        name="out_proj",
    )(*args)


def _row_spec(tm, width, col):
    return pl.BlockSpec((None, tm, width), lambda bi, i: (bi, i, col))


def _vec_spec(width):
    return pl.BlockSpec((1, width), lambda bi, i: (0, 0))


def _hgrn2_kernel(*refs, reverse, hb, nch, add):
    q_ref, f_ref, v_ref, lb_ref = refs[:4]
    prev_ref = refs[4] if add else None
    o_ref, st_ref = refs[-2:]

    @pl.when(pl.program_id(2) == 0)
    def _():
        st_ref[...] = jnp.zeros_like(st_ref)

    lb_exp = jnp.exp(lb_ref[...] - jnp.max(lb_ref[...], axis=0, keepdims=True))
    lb_all = lb_exp[0:1, :] / jnp.sum(lb_exp, axis=0, keepdims=True)

    c = HG_CHUNK
    row = lax.broadcasted_iota(jnp.int32, (c, c), 0)
    col = lax.broadcasted_iota(jnp.int32, (c, c), 1)
    mask = (col >= row) if reverse else (col <= row)
    rowk = lax.broadcasted_iota(jnp.int32, (c, HG_DK), 0)
    mid = c // 2
    ref_row = (c - 1 - mid) if reverse else mid
    last_row = 0 if reverse else c - 1

    def chunk(ci, carry):
        cc = (nch - 1 - ci) if reverse else ci
        r0 = pl.multiple_of(cc * c, c)
        for hh in range(hb):
            sl = slice(hh * HG_DK, (hh + 1) * HG_DK)
            q = _silu(q_ref[pl.ds(r0, c), sl])
            lb = lb_all[:, sl]
            f = lb + (1.0 - lb) * _sigmoid(f_ref[pl.ds(r0, c), sl])
            k = 1.0 - f
            g = jnp.log(f)
            v = v_ref[pl.ds(r0, c), sl]
            bsum = g
            s = 1
            while s < c:
                shifted = pltpu.roll(bsum, (c - s) if reverse else s, 0)
                valid = (rowk < c - s) if reverse else (rowk >= s)
                bsum = bsum + jnp.where(valid, shifted, 0.0)
                s *= 2
            b_ref_row = bsum[ref_row:ref_row + 1, :]
            b_last = bsum[last_row:last_row + 1, :]
            scores = _bdot_nt(q * jnp.exp(bsum - b_ref_row), k * jnp.exp(b_ref_row - bsum))
            scores = jnp.where(mask, scores, 0.0)
            st = st_ref[hh]
            vt = v.T
            o = _bdot_nt(jnp.concatenate([q * jnp.exp(bsum), scores], axis=1),
                         jnp.concatenate([st, vt], axis=1))
            st_ref[hh] = st * jnp.exp(b_last) + _bdot(vt, k * jnp.exp(b_last - bsum))
            if add:
                o = o + prev_ref[pl.ds(r0, c), sl]
            o_ref[pl.ds(r0, c), sl] = o
        return carry

    lax.fori_loop(0, nch, chunk, 0, unroll=2)


def _hgrn2_dir(proj, lb, reverse, prev):
    b, l, _ = proj.shape
    e = E_WIDTH
    hb = 8
    w = hb * HG_DK
    t = min(l, 512)
    nt = l // t
    ncol = e // w
    fsec = 2 if reverse else 1

    def rows(bi, h, ti):
        return (nt - 1 - ti) if reverse else ti

    def sec(s):
        return pl.BlockSpec((None, t, w), lambda bi, h, ti: (bi, rows(bi, h, ti), s * ncol + h))

    in_specs = [sec(0), sec(fsec), sec(3), pl.BlockSpec((DEPTH + 1, w), lambda bi, h, ti: (0, h))]
    args = [proj, proj, proj, lb]
    out_spec = pl.BlockSpec((None, t, w), lambda bi, h, ti: (bi, rows(bi, h, ti), h))
    if prev is not None:
        in_specs.append(out_spec)
        args.append(prev)
    return pl.pallas_call(
        functools.partial(_hgrn2_kernel, reverse=reverse, hb=hb, nch=t // HG_CHUNK,
                          add=prev is not None),
        grid=(b, ncol, nt),
        in_specs=in_specs,
        out_specs=out_spec,
        out_shape=jax.ShapeDtypeStruct((b, l, e), F32),
        scratch_shapes=[pltpu.VMEM((hb, HG_DK, HG_DK), F32)],
        compiler_params=_cparams(("arbitrary", "arbitrary", "arbitrary"), 32),
        name="hgrn2_bwd" if reverse else "hgrn2_fwd",
    )(*args)


def _ret_tables(reverse):
    c = RT_CHUNK
    hidx = np.arange(RT_HEADS, dtype=np.float64)
    lg = np.log1p(-np.exp2((-5.5 if reverse else -5.0) - hidx))[:, None]
    pos = np.arange(c, dtype=np.float64)[None, :]
    rel = pos[0][:, None] - pos[0][None, :]
    if reverse:
        rel = -rel
    decay = np.where(rel >= 0, np.exp(lg[:, :, None] * np.maximum(rel, 0.0)[None]), 0.0)
    q_dec = np.exp(lg * ((c - pos) if reverse else (pos + 1.0)))
    k_dec = np.exp(lg * (pos if reverse else (c - 1.0 - pos)))
    c_dec = np.exp(lg * c)
    return (jnp.asarray(decay, F32),
            jnp.asarray(np.broadcast_to(q_dec[:, :, None], (RT_HEADS, c, RT_DV)), F32),
            jnp.asarray(np.broadcast_to(k_dec[:, :, None], (RT_HEADS, c, RT_DK)), F32),
            jnp.asarray(np.broadcast_to(c_dec[:, :, None], (RT_HEADS, 1, RT_DV)), F32))


def _rope_tables(l):
    inv = RT_ROPE_BASE ** (-jnp.arange(0, RT_DK, 2, dtype=F32) / RT_DK)
    ang = jnp.arange(l, dtype=F32)[:, None] * inv[None]
    return jnp.cos(ang), jnp.sin(ang)


def _ret_kernel(*refs, add):
    q_ref, k_ref, v_ref, cos_ref, sin_ref, dec_ref, qd_ref, kd_ref, cd_ref = refs[:9]
    prev_ref = refs[9] if add else None
    o_ref, r_ref = refs[-2:]

    @pl.when(pl.program_id(1) == 0)
    def _():
        r_ref[...] = jnp.zeros_like(r_ref)

    cos = cos_ref[...]
    sin = sin_ref[...]
    half = RT_DK // 2

    def rot(t):
        t1 = t[:, :half]
        t2 = t[:, half:]
        return jnp.concatenate([t1 * cos - t2 * sin, t1 * sin + t2 * cos], axis=1)

    for h in range(RT_HEADS):
        qs = slice(h * RT_DK, (h + 1) * RT_DK)
        vs = slice(h * RT_DV, (h + 1) * RT_DV)
        q = rot(q_ref[:, qs])
        k = rot(k_ref[:, qs]) * (RT_DK ** -0.5)
        v = v_ref[:, vs]
        scores = _bdot_nt(q, k) * dec_ref[h]
        r = r_ref[h]
        o = _bdot(scores, v) + qd_ref[h] * _bdot(q, r)
        r_ref[h] = cd_ref[h] * r + _bdot((k * kd_ref[h]).T, v)
        if add:
            o = o + prev_ref[:, vs]
        o_ref[:, vs] = o


def _ret_dir(proj, cos, sin, reverse, prev):
    b, l, _ = proj.shape
    c = RT_CHUNK
    nt = l // c
    e = E_WIDTH
    dec, qd, kd, cd = _ret_tables(reverse)

    def rows(ti):
        return (nt - 1 - ti) if reverse else ti

    whole = lambda a: pl.BlockSpec(a.shape, lambda bi, ti: (0, 0, 0))
    in_specs = [
        pl.BlockSpec((None, c, RT_QK), lambda bi, ti: (bi, rows(ti), 0)),
        pl.BlockSpec((None, c, RT_QK), lambda bi, ti: (bi, rows(ti), 1)),
        pl.BlockSpec((None, c, e), lambda bi, ti: (bi, rows(ti), 2 * RT_QK // e)),
        pl.BlockSpec((c, RT_DK // 2), lambda bi, ti: (rows(ti), 0)),
        pl.BlockSpec((c, RT_DK // 2), lambda bi, ti: (rows(ti), 0)),
        whole(dec), whole(qd), whole(kd), whole(cd),
    ]
    args = [proj, proj, proj, cos, sin, dec, qd, kd, cd]
    out_spec = pl.BlockSpec((None, c, e), lambda bi, ti: (bi, rows(ti), 0))
    if prev is not None:
        in_specs.append(out_spec)
        args.append(prev)
    return pl.pallas_call(
        functools.partial(_ret_kernel, add=prev is not None),
        grid=(b, nt),
        in_specs=in_specs,
        out_specs=out_spec,
        out_shape=jax.ShapeDtypeStruct((b, l, e), F32),
        scratch_shapes=[pltpu.VMEM((RT_HEADS, RT_DK, RT_DV), F32)],
        compiler_params=_cparams(("arbitrary", "arbitrary"), 48),
        name="ret_bwd" if reverse else "ret_fwd",
    )(*args)


def _halo_specs(t, w, l, col, order):
    per = t // SUBLANES
    nblk = l // SUBLANES
    prev = pl.BlockSpec((None, SUBLANES, w),
                        lambda *g: (g[0], jnp.maximum(order(*g) * per - 1, 0), col(*g)))
    nxt = pl.BlockSpec((None, SUBLANES, w),
                       lambda *g: (g[0], jnp.minimum((order(*g) + 1) * per, nblk - 1), col(*g)))
    return prev, nxt


def _fill_ext(ext_ref, x_ref, xp_ref, xn_ref, first, last, t):
    ext_ref[0:SUBLANES, :] = jnp.where(first, 0.0, xp_ref[...])
    ext_ref[SUBLANES:SUBLANES + t, :] = x_ref[...]
    ext_ref[SUBLANES + t:2 * SUBLANES + t, :] = jnp.where(last, 0.0, xn_ref[...])


def _lru_scan(a, bb, carry, reverse):
    sub = lax.broadcasted_iota(jnp.int32, (SUBLANES, a.shape[1]), 0)
    groups = a.shape[0] // SUBLANES
    order = range(groups - 1, -1, -1) if reverse else range(groups)
    edge = 0 if reverse else SUBLANES - 1
    out = [None] * groups
    for gi in order:
        ag = a[gi * SUBLANES:(gi + 1) * SUBLANES, :]
        bg = bb[gi * SUBLANES:(gi + 1) * SUBLANES, :]
        s = 1
        while s < SUBLANES:
            valid = (sub < SUBLANES - s) if reverse else (sub >= s)
            shift = (SUBLANES - s) if reverse else s
            a_sh = jnp.where(valid, pltpu.roll(ag, shift, 0), 1.0)
            b_sh = jnp.where(valid, pltpu.roll(bg, shift, 0), 0.0)
            bg = ag * b_sh + bg
            ag = ag * a_sh
            s *= 2
        hg = bg + ag * carry
        carry = hg[edge:edge + 1, :]
        out[gi] = hg
    return jnp.concatenate(out, axis=0), carry


def _lru_kernel(*refs, reverse, t, nt, add):
    x_ref, xp_ref, xn_ref, cw_ref, cb_ref, gw_ref, gb_ref, lam_ref = refs[:8]
    prev_ref = refs[8] if add else None
    o_ref, ext_ref, carry_ref = refs[-3:]
    ti = pl.program_id(1)
    te = (nt - 1 - ti) if reverse else ti

    @pl.when(ti == 0)
    def _():
        carry_ref[...] = jnp.zeros_like(carry_ref)

    _fill_ext(ext_ref, x_ref, xp_ref, xn_ref, te == 0, te == nt - 1, t)
    left = LRU_CONV // 2
    xb = cb_ref[...]
    for j in range(LRU_CONV):
        xb = xb + cw_ref[j:j + 1, :] * ext_ref[pl.ds(SUBLANES - left + j, t), :]

    neg_lam = -lam_ref[...]
    softplus = jnp.maximum(neg_lam, 0.0) + jnp.log1p(jnp.exp(-jnp.abs(neg_lam)))
    for n in range(LRU_BLOCKS):
        sl = slice(n * LRU_BS, (n + 1) * LRU_BS)
        xn = xb[:, sl]
        gates = _bdot(xn, gw_ref[n]) + gb_ref[n]
        r = _sigmoid(gates[:, :LRU_BS])
        i = _sigmoid(gates[:, LRU_BS:])
        log_a = -LRU_C * r * softplus[:, sl]
        a = jnp.exp(log_a)
        bb = jnp.sqrt(-_expm1(2.0 * log_a)) * (i * xn)
        h, carry = _lru_scan(a, bb, carry_ref[:, sl], reverse)
        carry_ref[:, sl] = carry
        if add:
            h = h + prev_ref[:, sl]
        o_ref[:, sl] = h


def _lru_dir(proj, conv_w, conv_b, gate_w16, gate_b, lam, reverse, prev):
    b, l, _ = proj.shape
    e = E_WIDTH
    t = min(l, 256)
    nt = l // t

    def order(bi, ti):
        return (nt - 1 - ti) if reverse else ti

    xp_spec, xn_spec = _halo_specs(t, e, l, lambda bi, ti: 0, order)
    in_specs = [
        pl.BlockSpec((None, t, e), lambda bi, ti: (bi, order(bi, ti), 0)), xp_spec, xn_spec,
        pl.BlockSpec((LRU_CONV, e), lambda bi, ti: (0, 0)),
        pl.BlockSpec((1, e), lambda bi, ti: (0, 0)),
        pl.BlockSpec((LRU_BLOCKS, LRU_BS, 2 * LRU_BS), lambda bi, ti: (0, 0, 0)),
        pl.BlockSpec((LRU_BLOCKS, 1, 2 * LRU_BS), lambda bi, ti: (0, 0, 0)),
        pl.BlockSpec((1, e), lambda bi, ti: (0, 0)),
    ]
    args = [proj, proj, proj, conv_w, conv_b.reshape(1, e), gate_w16, gate_b, lam.reshape(1, e)]
    out_spec = pl.BlockSpec((None, t, e), lambda bi, ti: (bi, order(bi, ti), 0))
    if prev is not None:
        in_specs.append(out_spec)
        args.append(prev)
    return pl.pallas_call(
        functools.partial(_lru_kernel, reverse=reverse, t=t, nt=nt, add=prev is not None),
        grid=(b, nt),
        in_specs=in_specs,
        out_specs=out_spec,
        out_shape=jax.ShapeDtypeStruct((b, l, e), F32),
        scratch_shapes=[pltpu.VMEM((t + 2 * SUBLANES, e), F32), pltpu.VMEM((1, e), F32)],
        compiler_params=_cparams(("arbitrary", "arbitrary"), 48),
        name="lru_bwd" if reverse else "lru_fwd",
    )(*args)


def _hy_pre_kernel(x0_ref, x0p_ref, x0n_ref, x1_ref, x1p_ref, x1n_ref, v_ref, vp_ref, vn_ref,
                   z_ref, w0_ref, w1_ref, wv_ref, b0_ref, b1_ref, bv_ref,
                   u_ref, g1_ref, e0_ref, e1_ref, ev_ref, *, t, nt):
    ti = pl.program_id(1)
    first = ti == 0
    last = ti == nt - 1

    def conv(x_ref, xp_ref, xn_ref, ext_ref, w_ref, b_ref):
        _fill_ext(ext_ref, x_ref, xp_ref, xn_ref, first, last, t)
        out = b_ref[...]
        for j in range(3):
            out = out + w_ref[j:j + 1, :] * ext_ref[pl.ds(SUBLANES - 1 + j, t), :]
        return out

    x0 = conv(x0_ref, x0p_ref, x0n_ref, e0_ref, w0_ref, b0_ref)
    x1 = conv(x1_ref, x1p_ref, x1n_ref, e1_ref, w1_ref, b1_ref)
    v = conv(v_ref, vp_ref, vn_ref, ev_ref, wv_ref, bv_ref)
    u_ref[...] = x0 * v
    g1_ref[...] = x1 * _silu(z_ref[...])


def _hy_pre(proj, conv_w, conv_b):
    b, l, _ = proj.shape
    e = E_WIDTH
    w = 512
    t = min(l, 512)
    nt = l // t
    ncol = e // w

    def order(bi, ti, j):
        return ti

    specs, args = [], []
    for s in range(3):
        col = (lambda s: lambda bi, ti, j: s * ncol + j)(s)
        xp, xn = _halo_specs(t, w, l, col, order)
        specs += [pl.BlockSpec((None, t, w), (lambda col: lambda bi, ti, j: (bi, ti, col(bi, ti, j)))(col)),
                  xp, xn]
        args += [proj, proj, proj]
    specs.append(pl.BlockSpec((None, t, w), lambda bi, ti, j: (bi, ti, 3 * ncol + j)))
    args.append(proj)
    for s in range(3):
        specs.append(pl.BlockSpec((3, w), (lambda s: lambda bi, ti, j: (0, s * ncol + j))(s)))
        args.append(conv_w)
    cb = conv_b.reshape(1, 3 * e)
    for s in range(3):
        specs.append(pl.BlockSpec((1, w), (lambda s: lambda bi, ti, j: (0, s * ncol + j))(s)))
        args.append(cb)
    out_spec = pl.BlockSpec((None, t, w), lambda bi, ti, j: (bi, ti, j))
    return pl.pallas_call(
        functools.partial(_hy_pre_kernel, t=t, nt=nt),
        grid=(b, nt, ncol),
        in_specs=specs,
        out_specs=[out_spec, out_spec],
        out_shape=[jax.ShapeDtypeStruct((b, l, e), F32)] * 2,
        scratch_shapes=[pltpu.VMEM((t + 2 * SUBLANES, w), F32)] * 3,
        compiler_params=_cparams(("arbitrary", "arbitrary", "arbitrary"), 48),
        name="hy_pre",
    )(*args)


def _hy_filter_kernel(z_ref, w1_ref, b1_ref, w2_ref, b2_ref, fr_ref, wf_ref, wb_ref, dl_ref,
                      sd_ref, a_ref):
    @pl.when(pl.program_id(1) == 0)
    def _():
        fr = fr_ref[...]
        a = jnp.sin(fr * (_bdot(z_ref[...], w1_ref[...]) + b1_ref[...]))
        for j in range(HY_INNER):
            a = jnp.sin(fr * (_bdot(a, w2_ref[j]) + b2_ref[j]))
        a_ref[...] = a

    a = a_ref[...]
    window = jnp.exp(-z_ref[:, 0:1] * dl_ref[...])
    h_fw = _bdot(a, wf_ref[...]) * window
    h_bw = _bdot(a, wb_ref[...]) * window
    sd_ref[0] = h_fw + h_bw
    sd_ref[1] = h_fw - h_bw


def _hy_filters(l, w1, b1, w2, b2, wout16, freq):
    e = E_WIDTH
    kp = LANES
    t = jnp.linspace(0.0, 1.0, l, dtype=F32)[:, None]
    wv = 2.0 * math.pi * jnp.arange(l, dtype=F32)[:, None] / l
    bands = jnp.linspace(1e-4, HY_BANDS - 1, HY_BANDS, dtype=F32)[None]
    z = jnp.concatenate([t, jnp.cos(bands * wv), -jnp.sin(bands * wv),
                         jnp.zeros((l, kp - HY_EMB), F32)], axis=-1)
    w1p = jnp.concatenate([w1, jnp.zeros((kp - HY_EMB, HY_FH), F32)], axis=0)
    max_decay = math.log(HY_TARGET) / HY_FAST_DECAY
    min_decay = math.log(HY_TARGET) / HY_SLOW_DECAY
    deltas = jnp.abs(jnp.linspace(min_decay, max_decay, e, dtype=F32))[None]
    tm = min(l, 512)
    w = 512
    ncol = e // w
    return pl.pallas_call(
        _hy_filter_kernel,
        grid=(l // tm, ncol),
        in_specs=[pl.BlockSpec((tm, kp), lambda i, j: (i, 0)),
                  pl.BlockSpec((kp, HY_FH), lambda i, j: (0, 0)),
                  pl.BlockSpec((1, HY_FH), lambda i, j: (0, 0)),
                  pl.BlockSpec((HY_INNER, HY_FH, HY_FH), lambda i, j: (0, 0, 0)),
                  pl.BlockSpec((HY_INNER, 1, HY_FH), lambda i, j: (0, 0, 0)),
                  pl.BlockSpec((1, HY_FH), lambda i, j: (0, 0)),
                  pl.BlockSpec((HY_FH, w), lambda i, j: (0, j)),
                  pl.BlockSpec((HY_FH, w), lambda i, j: (0, ncol + j)),
                  pl.BlockSpec((1, w), lambda i, j: (0, j))],
        out_specs=pl.BlockSpec((2, tm, w), lambda i, j: (0, i, j)),
        out_shape=jax.ShapeDtypeStruct((2, l, e), F32),
        scratch_shapes=[pltpu.VMEM((tm, HY_FH), F32)],
        compiler_params=_cparams(("arbitrary", "arbitrary"), 32),
        name="hy_filter",
    )(z, w1p, b1.reshape(1, HY_FH), w2, b2.reshape(HY_INNER, 1, HY_FH), freq.reshape(1, HY_FH),
      wout16, wout16, deltas)


FFT_UNROLL = 16


def _fft_dims(l):
    n = 2 * l
    n1 = int(round(math.sqrt(n)))
    assert n1 * n1 == n and n1 % 16 == 0, "sequence length must give a square DFT factorisation"
    return n1, n1


def _fft_pitch(n1):
    return 2 * n1 + SUBLANES


def _fft_tables(l):
    n1, n2 = _fft_dims(l)
    n = n1 * n2
    k1 = np.arange(n1)[:, None]
    m1 = np.arange(n1 // 2)[None, :]
    j2 = np.arange(n2)[:, None, None]
    ang = -2.0 * np.pi * (k1 * m1 / n1)[None] - 2.0 * np.pi * (j2 * k1[None] / n)
    gr, gi = np.cos(ang), np.sin(ang)
    g_fwd = np.concatenate([gr, gi], axis=1)
    g_fwd2 = np.concatenate([np.concatenate([gr, -gi], axis=2), np.concatenate([gi, gr], axis=2)], axis=1)
    ang_i = 2.0 * np.pi * (m1.T * k1.T / n1)[None] + 2.0 * np.pi * (j2 * k1.T[None] / n)
    er, ei = np.cos(ang_i) / n, np.sin(ang_i) / n
    g_inv2 = np.concatenate([np.concatenate([er, -ei], axis=2), np.concatenate([ei, er], axis=2)], axis=1)
    a2 = -2.0 * np.pi * np.arange(n2)[:, None] * np.arange(n2)[None, :] / n2
    fr, fi = np.cos(a2), np.sin(a2)
    f2 = np.block([[fr, -fi], [fi, fr]])
    f2_inv = np.block([[fr, fi], [-fi, fr]])
    f2_half = np.stack([np.concatenate([fr, -fi], axis=1), np.concatenate([fi, fr], axis=1)])
    as16 = lambda a: jnp.asarray(a, F32).astype(BF16)
    return dict(g_fwd=as16(g_fwd), g_fwd2=as16(g_fwd2), g_inv2=as16(g_inv2), f2=as16(f2),
                f2_inv=as16(f2_inv), f2_half=as16(f2_half))


def _fft_stage1(x_ref, g_ref, work_ref, n2_lo, cnt, n1, n2, pitch):
    def body(j, carry):
        jj = n2_lo + j
        xs = x_ref[pl.ds(jj, n1 // 2, stride=n2), :]
        r0 = pl.multiple_of(jj * pitch, SUBLANES)
        work_ref[pl.ds(r0, 2 * n1), :] = _bdot(g_ref[j], xs)
        return carry
    lax.fori_loop(0, cnt, body, 0, unroll=FFT_UNROLL)


def _fft_load_k1(work_ref, k1, n1, n2, pitch):
    br = work_ref[pl.ds(k1, n2, stride=pitch), :]
    bi = work_ref[pl.ds(n1 + k1, n2, stride=pitch), :]
    return jnp.concatenate([br, bi], axis=0)


def _hy_spec_kernel(x_ref, g_ref, f2_ref, t_ref, work_ref, *, n1, n2, nc):
    p = pl.program_id(2)
    pitch = _fft_pitch(n1)
    c2 = n2 // nc
    c1 = n1 // nc

    @pl.when(p < nc)
    def _():
        _fft_stage1(x_ref, g_ref, work_ref, p * c2, c2, n1, n2, pitch)

    @pl.when(p >= nc)
    def _():
        def body(j, carry):
            k1 = (p - nc) * c1 + j
            t_ref[j] = _bdot(f2_ref[...], _fft_load_k1(work_ref, k1, n1, n2, pitch))
            return carry
        lax.fori_loop(0, c1, body, 0, unroll=FFT_UNROLL)


def _fft_nc(l):
    return 8 if l >= 8192 else (2 if l >= 2048 else 1)


def _hy_spectrum(sd, tables):
    _, l, e = sd.shape
    n1, n2 = _fft_dims(l)
    nc = _fft_nc(l)
    g_fwd, f2_half = tables['g_fwd'], tables['f2_half']
    pitch = _fft_pitch(n1)
    return pl.pallas_call(
        functools.partial(_hy_spec_kernel, n1=n1, n2=n2, nc=nc),
        grid=(e // LANES, 2, 2 * nc),
        in_specs=[pl.BlockSpec((None, l, LANES), lambda c, j, p: (j, 0, c)),
                  pl.BlockSpec((n2 // nc, 2 * n1, n1 // 2), lambda c, j, p: (jnp.minimum(p, nc - 1), 0, 0)),
                  pl.BlockSpec((None, n2, 2 * n2), lambda c, j, p: (j, 0, 0))],
        out_specs=pl.BlockSpec((n1 // nc, n2, LANES),
                               lambda c, j, p: (jnp.maximum(p - nc, 0), j, c)),
        out_shape=jax.ShapeDtypeStruct((n1, 2 * n2, e), F32),
        scratch_shapes=[pltpu.VMEM((n2 * pitch, LANES), F32)],
        compiler_params=_cparams(("arbitrary", "arbitrary", "arbitrary"), 48),
        name="hy_spectrum",
    )(sd, g_fwd, f2_half)


def _hy_conv_kernel(u_ref, gf_ref, f2_ref, f2i_ref, t_ref, gi_ref, y_ref, work_ref, *, n1, n2, nc):
    p = pl.program_id(2)
    pitch = _fft_pitch(n1)
    c2 = n2 // nc
    c1 = n1 // nc

    @pl.when(p < nc)
    def _():
        def body(j, carry):
            jj = p * c2 + j
            xs = jnp.concatenate([u_ref[0, pl.ds(jj, n1 // 2, stride=n2), :],
                                  u_ref[1, pl.ds(jj, n1 // 2, stride=n2), :]], axis=0)
            r0 = pl.multiple_of(jj * pitch, SUBLANES)
            work_ref[pl.ds(r0, 2 * n1), :] = _bdot(gf_ref[j], xs)
            return carry
        lax.fori_loop(0, c2, body, 0, unroll=FFT_UNROLL)

    @pl.when(jnp.logical_and(p >= nc, p < 2 * nc))
    def _():
        def body(j, carry):
            k1 = (p - nc) * c1 + j
            x = _bdot(f2_ref[...], _fft_load_k1(work_ref, k1, n1, n2, pitch))
            xr, xi = x[:n2], x[n2:]
            tr, ti = t_ref[j, :n2, :], t_ref[j, n2:, :]
            z = jnp.concatenate([xr * tr - xi * ti, xr * ti + xi * tr], axis=0)
            cmat = _bdot(f2i_ref[...], z)
            work_ref[pl.ds(k1, n2, stride=pitch), :] = cmat[:n2]
            work_ref[pl.ds(n1 + k1, n2, stride=pitch), :] = cmat[n2:]
            return carry
        lax.fori_loop(0, c1, body, 0, unroll=FFT_UNROLL)

    @pl.when(p >= 2 * nc)
    def _():
        def body(j, carry):
            jj = (p - 2 * nc) * c2 + j
            r0 = pl.multiple_of(jj * pitch, SUBLANES)
            d = work_ref[pl.ds(r0, 2 * n1), :]
            y = _bdot(gi_ref[j], d)
            y_ref[0, pl.ds(jj, n1 // 2, stride=n2), :] = y[:n1 // 2]
            y_ref[1, pl.ds(jj, n1 // 2, stride=n2), :] = y[n1 // 2:]
            return carry
        lax.fori_loop(0, c2, body, 0, unroll=FFT_UNROLL)


def _hy_conv(u, spec, tables):
    b, l, e = u.shape
    assert b % 2 == 0, "batch rows are transformed in pairs"
    n1, n2 = _fft_dims(l)
    nc = _fft_nc(l)
    pitch = _fft_pitch(n1)
    clip = lambda v: jnp.clip(v, 0, nc - 1)
    pair_spec = lambda **kw: pl.BlockSpec((2, l, LANES), lambda c, bi, p: (bi, 0, c), **kw)
    return pl.pallas_call(
        functools.partial(_hy_conv_kernel, n1=n1, n2=n2, nc=nc),
        grid=(e // LANES, b // 2, 3 * nc),
        in_specs=[pair_spec(),
                  pl.BlockSpec((n2 // nc, 2 * n1, n1), lambda c, bi, p: (clip(p), 0, 0)),
                  pl.BlockSpec((2 * n2, 2 * n2), lambda c, bi, p: (0, 0)),
                  pl.BlockSpec((2 * n2, 2 * n2), lambda c, bi, p: (0, 0)),
                  pl.BlockSpec((n1 // nc, 2 * n2, LANES), lambda c, bi, p: (clip(p - nc), 0, c)),
                  pl.BlockSpec((n2 // nc, n1, 2 * n1), lambda c, bi, p: (clip(p - 2 * nc), 0, 0))],
        out_specs=pair_spec(pipeline_mode=pl.Buffered(1)),
        out_shape=jax.ShapeDtypeStruct((b, l, e), F32),
        scratch_shapes=[pltpu.VMEM((n2 * pitch, LANES), F32)],
        compiler_params=_cparams(("arbitrary", "arbitrary", "arbitrary"), 56),
        name="hy_conv",
    )(u, tables['g_fwd2'], tables['f2'], tables['f2_inv'], spec, tables['g_inv2'])


def _prep_weights(p):
    c16 = lambda a: a.astype(BF16)
    gw = p['lru_gate_w'][0]
    gw = jnp.concatenate([gw[:, 0], gw[:, 1]], axis=-1)
    gb = p['lru_gate_b'][0].reshape(2, 2, LRU_BLOCKS, 1, LRU_BS)
    gb = jnp.concatenate([gb[:, 0], gb[:, 1]], axis=-1)
    lb = p['hg_lb'].astype(F32)
    return dict(
        ada_w=c16(p['ada_w']), hg_w_in=c16(p['hg_w_in'][0]), hg_w_out=c16(p['hg_w_out'][0]),
        hy_w_in=c16(p['hy_w_in'][0]), hy_w_out=c16(p['hy_w_out'][0]), hy_f_wout=c16(p['hy_f_wout'][0]),
        rt_w_in=c16(p['rt_w_in'][0]), rt_w_out=c16(p['rt_w_out'][0]),
        lru_w_in=c16(p['lru_w_in'][0]), lru_w_out=c16(p['lru_w_out'][0]),
        lru_gate_w=c16(gw), lru_gate_b=gb, hg_lb=lb,
        hg_norm_g=jnp.tile(p['hg_norm_g'][0], HG_HEADS))


def _trunk(x, mod, p, w):
    b, l, d = x.shape
    e = E_WIDTH
    x = x.astype(F32)
    zero_bias = lambda n: jnp.zeros((n,), F32)

    def split(layer):
        m = mod[layer]
        return m[:, :d], m[:, d:2 * d], m[:, 2 * d:]

    shift, scale, gate = split(0)
    proj = _in_proj(x, p['norm_g'][0], scale, shift, w['hg_w_in'], zero_bias(5 * e))
    o = _hgrn2_dir(proj, w['hg_lb'], False, None)
    o = _hgrn2_dir(proj, w['hg_lb'], True, o)
    x = _out_proj(_mix_hgrn2, [o, proj, w['hg_norm_g'].reshape(1, e)],
                  lambda tm: [_row_spec(tm, e, 0), _row_spec(tm, e, 4), _vec_spec(e)],
                  w['hg_w_out'], x, gate)

    shift, scale, gate = split(1)
    proj = _in_proj(x, p['norm_g'][1], scale, shift, w['hy_w_in'], p['hy_b_in'][0])
    u, g1 = _hy_pre(proj, p['hy_conv_w'][0], p['hy_conv_b'][0])
    tables = _fft_tables(l)
    sd = _hy_filters(l, p['hy_f_w1'][0], p['hy_f_b1'][0], p['hy_f_w2'][0], p['hy_f_b2'][0],
                     w['hy_f_wout'], p['hy_f_freq'][0])
    spec = _hy_spectrum(sd, tables)
    yc = _hy_conv(u, spec, tables)
    x = _out_proj(_mix_hyena, [yc, u, g1, p['hy_skip'][0].reshape(1, e)],
                  lambda tm: [_row_spec(tm, e, 0)] * 3 + [_vec_spec(e)],
                  w['hy_w_out'], x, gate)

    shift, scale, gate = split(2)
    proj = _in_proj(x, p['norm_g'][2], scale, shift, w['rt_w_in'], zero_bias(2 * RT_QK + 2 * e))
    cos, sin = _rope_tables(l)
    o = _ret_dir(proj, cos, sin, False, None)
    o = _ret_dir(proj, cos, sin, True, o)
    x = _out_proj(_mix_retention, [o, proj, p['rt_gn_g'][0].reshape(1, e)],
                  lambda tm: [_row_spec(tm, e, 0), _row_spec(tm, e, 2), _vec_spec(e)],
                  w['rt_w_out'], x, gate)

    shift, scale, gate = split(3)
    proj = _in_proj(x, p['norm_g'][3], scale, shift, w['lru_w_in'], zero_bias(2 * e))
    y = None
    for dirn in range(2):
        y = _lru_dir(proj, p['lru_conv_w'][0], p['lru_conv_b'][0], w['lru_gate_w'][dirn],
                     w['lru_gate_b'][dirn], p['lru_lambda'][0][dirn], dirn == 1, y)
    return _out_proj(_mix_lru, [y, proj], lambda tm: [_row_spec(tm, e, 0), _row_spec(tm, e, 1)],
                     w['lru_w_out'], x, gate, final_g=p['final_g'])


def kernel(x_prompt, x_sample, c_prompt, c_sample, ada_w, ada_b, norm_g, final_g, hg_lb, hg_w_in, hg_norm_g, hg_w_out, hy_w_in, hy_b_in, hy_conv_w, hy_conv_b, hy_f_w1, hy_f_b1, hy_f_w2, hy_f_b2, hy_f_wout, hy_f_freq, hy_skip, hy_w_out, rt_w_in, rt_gn_g, rt_w_out, lru_w_in, lru_conv_w, lru_conv_b, lru_gate_w, lru_gate_b, lru_lambda, lru_w_out):
    p = dict(ada_w=ada_w, ada_b=ada_b, norm_g=norm_g, final_g=final_g, hg_lb=hg_lb, hg_w_in=hg_w_in,
             hg_norm_g=hg_norm_g, hg_w_out=hg_w_out, hy_w_in=hy_w_in, hy_b_in=hy_b_in,
             hy_conv_w=hy_conv_w, hy_conv_b=hy_conv_b, hy_f_w1=hy_f_w1, hy_f_b1=hy_f_b1,
             hy_f_w2=hy_f_w2, hy_f_b2=hy_f_b2, hy_f_wout=hy_f_wout, hy_f_freq=hy_f_freq,
             hy_skip=hy_skip, hy_w_out=hy_w_out, rt_w_in=rt_w_in, rt_gn_g=rt_gn_g, rt_w_out=rt_w_out,
             lru_w_in=lru_w_in, lru_conv_w=lru_conv_w, lru_conv_b=lru_conv_b, lru_gate_w=lru_gate_w,
             lru_gate_b=lru_gate_b, lru_lambda=lru_lambda, lru_w_out=lru_w_out)
    w = _prep_weights(p)
    bp, bs = c_prompt.shape[0], c_sample.shape[0]
    rows = -(-(bp + bs) // SUBLANES) * SUBLANES
    c_all = jnp.concatenate([c_prompt, c_sample, jnp.zeros((rows - bp - bs, D_MODEL), F32)], axis=0)
    mod = _adaln(c_all.astype(F32), w['ada_w'], ada_b)
    y_prompt = _trunk(x_prompt, mod[:, :bp], p, w).astype(x_prompt.dtype)
    y_sample = _trunk(x_sample, mod[:, bp:bp + bs], p, w).astype(x_sample.dtype)
    return (y_prompt, y_sample)
```

```python
import functools
import math

import numpy as np
import jax
import jax.numpy as jnp
from jax import lax
from jax.experimental import pallas as pl
from jax.experimental.pallas import tpu as pltpu

F32 = jnp.float32
BF16 = jnp.bfloat16

D_MODEL = 1024
DEPTH = 4
E_WIDTH = 2 * D_MODEL
NORM_EPS = 1e-6
LANES = 128
SUBLANES = 8
MIB = 1024 * 1024
MIX_DTYPE = BF16
PROJ_DTYPE_NARROW = BF16

HG_CHUNK = 64
HG_DK = 128
HG_HEADS = E_WIDTH // HG_DK

HY_EMB = 33
HY_BANDS = 16
HY_FH = 64
HY_INNER = 2
HY_FAST_DECAY = 0.3
HY_SLOW_DECAY = 1.5
HY_TARGET = 1e-2

RT_HEADS = 4
RT_QK = D_MODEL
RT_DK = RT_QK // RT_HEADS
RT_DV = E_WIDTH // RT_HEADS
RT_ROPE_BASE = 10000.0
RT_CHUNK = 256

LRU_CONV = 4
LRU_BLOCKS = 16
LRU_BS = E_WIDTH // LRU_BLOCKS
LRU_C = 8.0

_NT = (((1,), (1,)), ((), ()))


def _cparams(sem, vmem_mib):
    return pltpu.CompilerParams(dimension_semantics=sem, vmem_limit_bytes=vmem_mib * MIB)


def _bdot(a, b):
    return jnp.dot(a.astype(BF16), b.astype(BF16), preferred_element_type=F32)


def _bdot_nt(a, b):
    return lax.dot_general(a.astype(BF16), b.astype(BF16), _NT, preferred_element_type=F32)


def _sigmoid(x):
    return jax.nn.sigmoid(x)


def _silu(x):
    return x * _sigmoid(x)


def _expm1(x):
    u = jnp.exp(x)
    plain = jnp.logical_or(u == 1.0, x < -0.5)
    small = (u - 1.0) * x / jnp.where(plain, 1.0, jnp.log(u))
    return jnp.where(u == 1.0, x, jnp.where(x < -0.5, u - 1.0, small))


def _adaln_kernel(c_ref, w_ref, b_ref, o_ref):
    cs = _silu(c_ref[...])
    o_ref[...] = _bdot(cs, w_ref[...]) + b_ref[...]


def _adaln(c_all, ada_w16, ada_b):
    bp, d = c_all.shape
    tn = 1024
    return pl.pallas_call(
        _adaln_kernel,
        grid=(DEPTH, 3 * d // tn),
        in_specs=[pl.BlockSpec((bp, d), lambda l, j: (0, 0)),
                  pl.BlockSpec((None, d, tn), lambda l, j: (l, 0, j)),
                  pl.BlockSpec((None, 1, tn), lambda l, j: (l, 0, j))],
        out_specs=pl.BlockSpec((None, bp, tn), lambda l, j: (l, 0, j)),
        out_shape=jax.ShapeDtypeStruct((DEPTH, bp, 3 * d), F32),
        compiler_params=_cparams(("arbitrary", "arbitrary"), 32),
        name="adaln",
    )(c_all, ada_w16, ada_b.reshape(DEPTH, 1, 3 * d))


def _in_proj_kernel(x_ref, g_ref, sc_ref, sh_ref, w_ref, b_ref, o_ref, h_ref, *, tm):
    r0 = pl.multiple_of(pl.program_id(2) * tm, tm)

    @pl.when(pl.program_id(1) == 0)
    def _():
        x = x_ref[...]
        ms = jnp.mean(x * x, axis=-1, keepdims=True)
        h = x * lax.rsqrt(ms + NORM_EPS) * g_ref[...] * (1.0 + sc_ref[...]) + sh_ref[...]
        h_ref[pl.ds(r0, tm), :] = h.astype(BF16)

    o_ref[...] = (jnp.dot(h_ref[pl.ds(r0, tm), :], w_ref[...], preferred_element_type=F32)
                  + b_ref[...]).astype(o_ref.dtype)


IN_PROJ_TM = 1024
IN_PROJ_TN = 1024


def _in_proj(x, norm_g, scale, shift, w16, bias, out_dtype=F32):
    b, l, d = x.shape
    p = w16.shape[1]
    tm = min(l, IN_PROJ_TM)
    tn = IN_PROJ_TN
    ni = l // tm
    x_rows = lambda bi, j, i: (bi, jnp.where(j == 0, i, ni - 1), 0)
    return pl.pallas_call(
        functools.partial(_in_proj_kernel, tm=tm),
        grid=(b, p // tn, ni),
        in_specs=[pl.BlockSpec((None, tm, d), x_rows),
                  pl.BlockSpec((1, d), lambda bi, j, i: (0, 0)),
                  pl.BlockSpec((None, 1, d), lambda bi, j, i: (bi, 0, 0)),
                  pl.BlockSpec((None, 1, d), lambda bi, j, i: (bi, 0, 0)),
                  pl.BlockSpec((d, tn), lambda bi, j, i: (0, j)),
                  pl.BlockSpec((1, tn), lambda bi, j, i: (0, j))],
        out_specs=pl.BlockSpec((None, tm, tn), lambda bi, j, i: (bi, i, j)),
        out_shape=jax.ShapeDtypeStruct((b, l, p), out_dtype),
        scratch_shapes=[pltpu.VMEM((l, d), BF16)],
        compiler_params=_cparams(("arbitrary", "arbitrary", "arbitrary"), 48),
        name="in_proj",
    )(x, norm_g.reshape(1, d), scale.reshape(b, 1, d), shift.reshape(b, 1, d), w16,
      bias.reshape(1, p))


def _ld(ref):
    return ref[...].astype(F32)


def _head_rms(o, width):
    parts = []
    for s in range(0, o.shape[1], width):
        oh = o[:, s:s + width]
        ms = jnp.mean(oh * oh, axis=-1, keepdims=True)
        parts.append(oh * lax.rsqrt(ms + NORM_EPS))
    return jnp.concatenate(parts, axis=1)


def _mix_hgrn2(o_ref, z_ref, g_ref):
    return (_head_rms(_ld(o_ref), HG_DK) * g_ref[...]) * _silu(_ld(z_ref))


def _mix_hyena(yc_ref, u_ref, g1_ref, skip_ref):
    return _ld(g1_ref) * (_ld(yc_ref) + _ld(u_ref) * skip_ref[...])


def _mix_retention(o_ref, z_ref, g_ref):
    return (_head_rms(_ld(o_ref), RT_DV) * g_ref[...]) * _silu(_ld(z_ref))


def _mix_lru(y_ref, z_ref):
    return _ld(y_ref) * _silu(_ld(z_ref))


def _out_proj_kernel(*refs, mix, n_mix, final):
    mix_refs = refs[:n_mix]
    w_ref, x_ref, gate_ref = refs[n_mix:n_mix + 3]
    o_ref = refs[-1]
    y = mix(*mix_refs)
    out = x_ref[...] + gate_ref[...] * _bdot(y, w_ref[...])
    if final:
        fg_ref = refs[n_mix + 3]
        ms = jnp.mean(out * out, axis=-1, keepdims=True)
        out = out * lax.rsqrt(ms + NORM_EPS) * fg_ref[...]
    o_ref[...] = out


def _out_proj(mix, mix_args, mix_specs, w16, x, gate, final_g=None):
    b, l, d = x.shape
    e = w16.shape[0]
    tm = min(l, 256)
    in_specs = list(mix_specs(tm)) + [
        pl.BlockSpec((e, d), lambda bi, i: (0, 0)),
        pl.BlockSpec((None, tm, d), lambda bi, i: (bi, i, 0)),
        pl.BlockSpec((None, 1, d), lambda bi, i: (bi, 0, 0))]
    args = list(mix_args) + [w16, x, gate.reshape(b, 1, d)]
    if final_g is not None:
        in_specs.append(pl.BlockSpec((1, d), lambda bi, i: (0, 0)))
        args.append(final_g.reshape(1, d))
    return pl.pallas_call(
        functools.partial(_out_proj_kernel, mix=mix, n_mix=len(mix_args), final=final_g is not None),
        grid=(b, l // tm),
        in_specs=in_specs,
        out_specs=pl.BlockSpec((None, tm, d), lambda bi, i: (bi, i, 0)),
        out_shape=jax.ShapeDtypeStruct((b, l, d), F32),
        compiler_params=_cparams(("arbitrary", "arbitrary"), 48),
        name="out_proj",
    )(*args)


def _row_spec(tm, width, col):
    return pl.BlockSpec((None, tm, width), lambda bi, i: (bi, i, col))


def _vec_spec(width):
    return pl.BlockSpec((1, width), lambda bi, i: (0, 0))


def _cumsum_rows(x, reverse):
    sub = lax.broadcasted_iota(jnp.int32, (SUBLANES, x.shape[1]), 0)
    groups = x.shape[0] // SUBLANES
    order = range(groups - 1, -1, -1) if reverse else range(groups)
    edge = 0 if reverse else SUBLANES - 1
    out = [None] * groups
    total = None
    for gi in order:
        xg = x[gi * SUBLANES:(gi + 1) * SUBLANES, :]
        s = 1
        while s < SUBLANES:
            valid = (sub < SUBLANES - s) if reverse else (sub >= s)
            xg = xg + jnp.where(valid, pltpu.roll(xg, (SUBLANES - s) if reverse else s, 0), 0.0)
            s *= 2
        if total is not None:
            xg = xg + total
        total = xg[edge:edge + 1, :]
        out[gi] = xg
    return jnp.concatenate(out, axis=0)


def _hgrn2_kernel(*refs, reverse, hb, nch, add):
    q_ref, f_ref, v_ref, lb_ref = refs[:4]
    prev_ref = refs[4] if add else None
    o_ref, st_ref = refs[-2:]

    @pl.when(pl.program_id(2) == 0)
    def _():
        st_ref[...] = jnp.zeros_like(st_ref)

    lb_exp = jnp.exp(lb_ref[...] - jnp.max(lb_ref[...], axis=0, keepdims=True))
    lb_all = lb_exp[0:1, :] / jnp.sum(lb_exp, axis=0, keepdims=True)

    c = HG_CHUNK
    row = lax.broadcasted_iota(jnp.int32, (c, c), 0)
    col = lax.broadcasted_iota(jnp.int32, (c, c), 1)
    mask = (col >= row) if reverse else (col <= row)
    mid = c // 2
    ref_row = (c - 1 - mid) if reverse else mid
    last_row = 0 if reverse else c - 1

    def chunk(ci, carry):
        cc = (nch - 1 - ci) if reverse else ci
        r0 = pl.multiple_of(cc * c, c)
        for hh in range(hb):
            sl = slice(hh * HG_DK, (hh + 1) * HG_DK)
            q = _silu(q_ref[pl.ds(r0, c), sl])
            lb = lb_all[:, sl]
            f = lb + (1.0 - lb) * _sigmoid(f_ref[pl.ds(r0, c), sl])
            k = 1.0 - f
            g = jnp.log(f)
            v = v_ref[pl.ds(r0, c), sl]
            bsum = _cumsum_rows(g, reverse)
            b_ref_row = bsum[ref_row:ref_row + 1, :]
            b_last = bsum[last_row:last_row + 1, :]
            scores = _bdot_nt(q * jnp.exp(bsum - b_ref_row), k * jnp.exp(b_ref_row - bsum))
            scores = jnp.where(mask, scores, 0.0)
            st = st_ref[hh]
            vt = v.T
            o = _bdot_nt(jnp.concatenate([q * jnp.exp(bsum), scores], axis=1),
                         jnp.concatenate([st, vt], axis=1))
            st_ref[hh] = st * jnp.exp(b_last) + _bdot(vt, k * jnp.exp(b_last - bsum))
            if add:
                o = o + prev_ref[pl.ds(r0, c), sl].astype(F32)
            o_ref[pl.ds(r0, c), sl] = o.astype(o_ref.dtype)
        return carry

    lax.fori_loop(0, nch, chunk, 0, unroll=2)


def _hgrn2_dir(proj, lb, reverse, prev):
    b, l, _ = proj.shape
    e = E_WIDTH
    hb = 8
    w = hb * HG_DK
    t = min(l, 512)
    nt = l // t
    ncol = e // w
    fsec = 2 if reverse else 1

    def rows(bi, h, ti):
        return (nt - 1 - ti) if reverse else ti

    def sec(s):
        return pl.BlockSpec((None, t, w), lambda bi, h, ti: (bi, rows(bi, h, ti), s * ncol + h))

    in_specs = [sec(0), sec(fsec), sec(3), pl.BlockSpec((DEPTH + 1, w), lambda bi, h, ti: (0, h))]
    args = [proj, proj, proj, lb]
    out_spec = pl.BlockSpec((None, t, w), lambda bi, h, ti: (bi, rows(bi, h, ti), h))
    if prev is not None:
        in_specs.append(out_spec)
        args.append(prev)
    return pl.pallas_call(
        functools.partial(_hgrn2_kernel, reverse=reverse, hb=hb, nch=t // HG_CHUNK,
                          add=prev is not None),
        grid=(b, ncol, nt),
        in_specs=in_specs,
        out_specs=out_spec,
        out_shape=jax.ShapeDtypeStruct((b, l, e), MIX_DTYPE),
        scratch_shapes=[pltpu.VMEM((hb, HG_DK, HG_DK), F32)],
        compiler_params=_cparams(("arbitrary", "arbitrary", "arbitrary"), 32),
        name="hgrn2_bwd" if reverse else "hgrn2_fwd",
    )(*args)


def _ret_tables(reverse):
    c = RT_CHUNK
    hidx = np.arange(RT_HEADS, dtype=np.float64)
    lg = np.log1p(-np.exp2((-5.5 if reverse else -5.0) - hidx))[:, None]
    pos = np.arange(c, dtype=np.float64)[None, :]
    rel = pos[0][:, None] - pos[0][None, :]
    if reverse:
        rel = -rel
    decay = np.where(rel >= 0, np.exp(lg[:, :, None] * np.maximum(rel, 0.0)[None]), 0.0)
    q_dec = np.exp(lg * ((c - pos) if reverse else (pos + 1.0)))
    k_dec = np.exp(lg * (pos if reverse else (c - 1.0 - pos)))
    c_dec = np.exp(lg * c)
    return (jnp.asarray(decay, F32),
            jnp.asarray(np.broadcast_to(q_dec[:, :, None], (RT_HEADS, c, RT_DV)), F32),
            jnp.asarray(np.broadcast_to(k_dec[:, :, None], (RT_HEADS, c, RT_DK)), F32),
            jnp.asarray(np.broadcast_to(c_dec[:, :, None], (RT_HEADS, 1, RT_DV)), F32))


def _rope_tables(l):
    inv = RT_ROPE_BASE ** (-jnp.arange(0, RT_DK, 2, dtype=F32) / RT_DK)
    ang = jnp.arange(l, dtype=F32)[:, None] * inv[None]
    return jnp.cos(ang), jnp.sin(ang)


def _ret_kernel(*refs, add):
    q_ref, k_ref, v_ref, cos_ref, sin_ref, dec_ref, qd_ref, kd_ref, cd_ref = refs[:9]
    prev_ref = refs[9] if add else None
    o_ref, r_ref = refs[-2:]

    @pl.when(pl.program_id(1) == 0)
    def _():
        r_ref[...] = jnp.zeros_like(r_ref)

    cos = cos_ref[...]
    sin = sin_ref[...]
    half = RT_DK // 2

    def rot(t):
        t1 = t[:, :half]
        t2 = t[:, half:]
        return jnp.concatenate([t1 * cos - t2 * sin, t1 * sin + t2 * cos], axis=1)

    for h in range(RT_HEADS):
        qs = slice(h * RT_DK, (h + 1) * RT_DK)
        vs = slice(h * RT_DV, (h + 1) * RT_DV)
        q = rot(q_ref[:, qs].astype(F32))
        k = rot(k_ref[:, qs].astype(F32)) * (RT_DK ** -0.5)
        v = v_ref[:, vs]
        scores = _bdot_nt(q, k) * dec_ref[h]
        r = r_ref[h]
        o = _bdot(scores, v) + qd_ref[h] * _bdot(q, r)
        r_ref[h] = cd_ref[h] * r + _bdot((k * kd_ref[h]).T, v)
        if add:
            o = o + prev_ref[:, vs].astype(F32)
        o_ref[:, vs] = o.astype(o_ref.dtype)


def _ret_dir(proj, cos, sin, reverse, prev):
    b, l, _ = proj.shape
    c = RT_CHUNK
    nt = l // c
    e = E_WIDTH
    dec, qd, kd, cd = _ret_tables(reverse)

    def rows(ti):
        return (nt - 1 - ti) if reverse else ti

    whole = lambda a: pl.BlockSpec(a.shape, lambda bi, ti: (0, 0, 0))
    in_specs = [
        pl.BlockSpec((None, c, RT_QK), lambda bi, ti: (bi, rows(ti), 0)),
        pl.BlockSpec((None, c, RT_QK), lambda bi, ti: (bi, rows(ti), 1)),
        pl.BlockSpec((None, c, e), lambda bi, ti: (bi, rows(ti), 2 * RT_QK // e)),
        pl.BlockSpec((c, RT_DK // 2), lambda bi, ti: (rows(ti), 0)),
        pl.BlockSpec((c, RT_DK // 2), lambda bi, ti: (rows(ti), 0)),
        whole(dec), whole(qd), whole(kd), whole(cd),
    ]
    args = [proj, proj, proj, cos, sin, dec, qd, kd, cd]
    out_spec = pl.BlockSpec((None, c, e), lambda bi, ti: (bi, rows(ti), 0))
    if prev is not None:
        in_specs.append(out_spec)
        args.append(prev)
    return pl.pallas_call(
        functools.partial(_ret_kernel, add=prev is not None),
        grid=(b, nt),
        in_specs=in_specs,
        out_specs=out_spec,
        out_shape=jax.ShapeDtypeStruct((b, l, e), MIX_DTYPE),
        scratch_shapes=[pltpu.VMEM((RT_HEADS, RT_DK, RT_DV), F32)],
        compiler_params=_cparams(("arbitrary", "arbitrary"), 48),
        name="ret_bwd" if reverse else "ret_fwd",
    )(*args)


def _halo_rows(dtype):
    return SUBLANES * (4 // jnp.dtype(dtype).itemsize)


def _halo_specs(t, w, l, col, order, hr=SUBLANES):
    per = t // hr
    nblk = l // hr
    prev = pl.BlockSpec((None, hr, w),
                        lambda *g: (g[0], jnp.maximum(order(*g) * per - 1, 0), col(*g)))
    nxt = pl.BlockSpec((None, hr, w),
                       lambda *g: (g[0], jnp.minimum((order(*g) + 1) * per, nblk - 1), col(*g)))
    return prev, nxt


def _fill_ext(ext_ref, x_ref, xp_ref, xn_ref, first, last, t):
    hr = xp_ref.shape[0]
    ext_ref[0:SUBLANES, :] = jnp.where(first, 0.0, xp_ref[hr - SUBLANES:hr, :].astype(F32))
    ext_ref[SUBLANES:SUBLANES + t, :] = x_ref[...].astype(F32)
    ext_ref[SUBLANES + t:2 * SUBLANES + t, :] = jnp.where(last, 0.0, xn_ref[0:SUBLANES, :].astype(F32))


def _lru_scan(a, bb, carry, reverse):
    sub = lax.broadcasted_iota(jnp.int32, (SUBLANES, a.shape[1]), 0)
    groups = a.shape[0] // SUBLANES
    order = range(groups - 1, -1, -1) if reverse else range(groups)
    edge = 0 if reverse else SUBLANES - 1
    out = [None] * groups
    for gi in order:
        ag = a[gi * SUBLANES:(gi + 1) * SUBLANES, :]
        bg = bb[gi * SUBLANES:(gi + 1) * SUBLANES, :]
        s = 1
        while s < SUBLANES:
            valid = (sub < SUBLANES - s) if reverse else (sub >= s)
            shift = (SUBLANES - s) if reverse else s
            a_sh = jnp.where(valid, pltpu.roll(ag, shift, 0), 1.0)
            b_sh = jnp.where(valid, pltpu.roll(bg, shift, 0), 0.0)
            bg = ag * b_sh + bg
            ag = ag * a_sh
            s *= 2
        hg = bg + ag * carry
        carry = hg[edge:edge + 1, :]
        out[gi] = hg
    return jnp.concatenate(out, axis=0), carry


def _lru_kernel(*refs, reverse, t, nt, add):
    x_ref, xp_ref, xn_ref, cw_ref, cb_ref, gw_ref, gb_ref, lam_ref = refs[:8]
    prev_ref = refs[8] if add else None
    o_ref, ext_ref, carry_ref = refs[-3:]
    ti = pl.program_id(1)
    te = (nt - 1 - ti) if reverse else ti

    @pl.when(ti == 0)
    def _():
        carry_ref[...] = jnp.zeros_like(carry_ref)

    _fill_ext(ext_ref, x_ref, xp_ref, xn_ref, te == 0, te == nt - 1, t)
    left = LRU_CONV // 2
    xb = cb_ref[...]
    for j in range(LRU_CONV):
        xb = xb + cw_ref[j:j + 1, :] * ext_ref[pl.ds(SUBLANES - left + j, t), :]

    neg_lam = -lam_ref[...]
    softplus = jnp.maximum(neg_lam, 0.0) + jnp.log1p(jnp.exp(-jnp.abs(neg_lam)))
    for n in range(LRU_BLOCKS):
        sl = slice(n * LRU_BS, (n + 1) * LRU_BS)
        xn = xb[:, sl]
        gates = _bdot(xn, gw_ref[n]) + gb_ref[n]
        r = _sigmoid(gates[:, :LRU_BS])
        i = _sigmoid(gates[:, LRU_BS:])
        log_a = -LRU_C * r * softplus[:, sl]
        a = jnp.exp(log_a)
        bb = jnp.sqrt(-_expm1(2.0 * log_a)) * (i * xn)
        h, carry = _lru_scan(a, bb, carry_ref[:, sl], reverse)
        carry_ref[:, sl] = carry
        if add:
            h = h + prev_ref[:, sl].astype(F32)
        o_ref[:, sl] = h.astype(o_ref.dtype)


def _lru_dir(proj, conv_w, conv_b, gate_w16, gate_b, lam, reverse, prev):
    b, l, _ = proj.shape
    e = E_WIDTH
    t = min(l, 256)
    nt = l // t

    def order(bi, ti):
        return (nt - 1 - ti) if reverse else ti

    xp_spec, xn_spec = _halo_specs(t, e, l, lambda bi, ti: 0, order)
    in_specs = [
        pl.BlockSpec((None, t, e), lambda bi, ti: (bi, order(bi, ti), 0)), xp_spec, xn_spec,
        pl.BlockSpec((LRU_CONV, e), lambda bi, ti: (0, 0)),
        pl.BlockSpec((1, e), lambda bi, ti: (0, 0)),
        pl.BlockSpec((LRU_BLOCKS, LRU_BS, 2 * LRU_BS), lambda bi, ti: (0, 0, 0)),
        pl.BlockSpec((LRU_BLOCKS, 1, 2 * LRU_BS), lambda bi, ti: (0, 0, 0)),
        pl.BlockSpec((1, e), lambda bi, ti: (0, 0)),
    ]
    args = [proj, proj, proj, conv_w, conv_b.reshape(1, e), gate_w16, gate_b, lam.reshape(1, e)]
    out_spec = pl.BlockSpec((None, t, e), lambda bi, ti: (bi, order(bi, ti), 0))
    if prev is not None:
        in_specs.append(out_spec)
        args.append(prev)
    return pl.pallas_call(
        functools.partial(_lru_kernel, reverse=reverse, t=t, nt=nt, add=prev is not None),
        grid=(b, nt),
        in_specs=in_specs,
        out_specs=out_spec,
        out_shape=jax.ShapeDtypeStruct((b, l, e), MIX_DTYPE),
        scratch_shapes=[pltpu.VMEM((t + 2 * SUBLANES, e), F32), pltpu.VMEM((1, e), F32)],
        compiler_params=_cparams(("arbitrary", "arbitrary"), 48),
        name="lru_bwd" if reverse else "lru_fwd",
    )(*args)


def _hy_pre_kernel(x0_ref, x0p_ref, x0n_ref, x1_ref, x1p_ref, x1n_ref, v_ref, vp_ref, vn_ref,
                   z_ref, w0_ref, w1_ref, wv_ref, b0_ref, b1_ref, bv_ref,
                   u_ref, g1_ref, e0_ref, e1_ref, ev_ref, *, t, nt):
    ti = pl.program_id(1)
    first = ti == 0
    last = ti == nt - 1

    def conv(x_ref, xp_ref, xn_ref, ext_ref, w_ref, b_ref):
        _fill_ext(ext_ref, x_ref, xp_ref, xn_ref, first, last, t)
        out = b_ref[...]
        for j in range(3):
            out = out + w_ref[j:j + 1, :] * ext_ref[pl.ds(SUBLANES - 1 + j, t), :]
        return out

    x0 = conv(x0_ref, x0p_ref, x0n_ref, e0_ref, w0_ref, b0_ref)
    x1 = conv(x1_ref, x1p_ref, x1n_ref, e1_ref, w1_ref, b1_ref)
    v = conv(v_ref, vp_ref, vn_ref, ev_ref, wv_ref, bv_ref)
    u_ref[...] = x0 * v
    g1_ref[...] = (x1 * _silu(z_ref[...].astype(F32))).astype(g1_ref.dtype)


def _hy_pre(proj, conv_w, conv_b):
    b, l, _ = proj.shape
    e = E_WIDTH
    w = 512
    t = min(l, 512)
    nt = l // t
    ncol = e // w

    def order(bi, ti, j):
        return ti

    specs, args = [], []
    for s in range(3):
        col = (lambda s: lambda bi, ti, j: s * ncol + j)(s)
        xp, xn = _halo_specs(t, w, l, col, order, _halo_rows(proj.dtype))
        specs += [pl.BlockSpec((None, t, w), (lambda col: lambda bi, ti, j: (bi, ti, col(bi, ti, j)))(col)),
                  xp, xn]
        args += [proj, proj, proj]
    specs.append(pl.BlockSpec((None, t, w), lambda bi, ti, j: (bi, ti, 3 * ncol + j)))
    args.append(proj)
    for s in range(3):
        specs.append(pl.BlockSpec((3, w), (lambda s: lambda bi, ti, j: (0, s * ncol + j))(s)))
        args.append(conv_w)
    cb = conv_b.reshape(1, 3 * e)
    for s in range(3):
        specs.append(pl.BlockSpec((1, w), (lambda s: lambda bi, ti, j: (0, s * ncol + j))(s)))
        args.append(cb)
    out_spec = pl.BlockSpec((None, t, w), lambda bi, ti, j: (bi, ti, j))
    return pl.pallas_call(
        functools.partial(_hy_pre_kernel, t=t, nt=nt),
        grid=(b, nt, ncol),
        in_specs=specs,
        out_specs=[out_spec, out_spec],
        out_shape=[jax.ShapeDtypeStruct((b, l, e), F32), jax.ShapeDtypeStruct((b, l, e), MIX_DTYPE)],
        scratch_shapes=[pltpu.VMEM((t + 2 * SUBLANES, w), F32)] * 3,
        compiler_params=_cparams(("arbitrary", "arbitrary", "arbitrary"), 48),
        name="hy_pre",
    )(*args)


def _hy_filter_kernel(z_ref, w1_ref, b1_ref, w2_ref, b2_ref, fr_ref, wf_ref, wb_ref, dl_ref,
                      sd_ref, a_ref):
    @pl.when(pl.program_id(1) == 0)
    def _():
        fr = fr_ref[...]
        a = jnp.sin(fr * (_bdot(z_ref[...], w1_ref[...]) + b1_ref[...]))
        for j in range(HY_INNER):
            a = jnp.sin(fr * (_bdot(a, w2_ref[j]) + b2_ref[j]))
        a_ref[...] = a

    a = a_ref[...]
    window = jnp.exp(-z_ref[:, 0:1] * dl_ref[...])
    h_fw = _bdot(a, wf_ref[...]) * window
    h_bw = _bdot(a, wb_ref[...]) * window
    sd_ref[0] = h_fw + h_bw
    sd_ref[1] = h_fw - h_bw


def _hy_filters(l, w1, b1, w2, b2, wout16, freq):
    e = E_WIDTH
    kp = LANES
    t = jnp.linspace(0.0, 1.0, l, dtype=F32)[:, None]
    wv = 2.0 * math.pi * jnp.arange(l, dtype=F32)[:, None] / l
    bands = jnp.linspace(1e-4, HY_BANDS - 1, HY_BANDS, dtype=F32)[None]
    z = jnp.concatenate([t, jnp.cos(bands * wv), -jnp.sin(bands * wv),
                         jnp.zeros((l, kp - HY_EMB), F32)], axis=-1)
    w1p = jnp.concatenate([w1, jnp.zeros((kp - HY_EMB, HY_FH), F32)], axis=0)
    max_decay = math.log(HY_TARGET) / HY_FAST_DECAY
    min_decay = math.log(HY_TARGET) / HY_SLOW_DECAY
    deltas = jnp.abs(jnp.linspace(min_decay, max_decay, e, dtype=F32))[None]
    tm = min(l, 512)
    w = 512
    ncol = e // w
    return pl.pallas_call(
        _hy_filter_kernel,
        grid=(l // tm, ncol),
        in_specs=[pl.BlockSpec((tm, kp), lambda i, j: (i, 0)),
                  pl.BlockSpec((kp, HY_FH), lambda i, j: (0, 0)),
                  pl.BlockSpec((1, HY_FH), lambda i, j: (0, 0)),
                  pl.BlockSpec((HY_INNER, HY_FH, HY_FH), lambda i, j: (0, 0, 0)),
                  pl.BlockSpec((HY_INNER, 1, HY_FH), lambda i, j: (0, 0, 0)),
                  pl.BlockSpec((1, HY_FH), lambda i, j: (0, 0)),
                  pl.BlockSpec((HY_FH, w), lambda i, j: (0, j)),
                  pl.BlockSpec((HY_FH, w), lambda i, j: (0, ncol + j)),
                  pl.BlockSpec((1, w), lambda i, j: (0, j))],
        out_specs=pl.BlockSpec((2, tm, w), lambda i, j: (0, i, j)),
        out_shape=jax.ShapeDtypeStruct((2, l, e), F32),
        scratch_shapes=[pltpu.VMEM((tm, HY_FH), F32)],
        compiler_params=_cparams(("arbitrary", "arbitrary"), 32),
        name="hy_filter",
    )(z, w1p, b1.reshape(1, HY_FH), w2, b2.reshape(HY_INNER, 1, HY_FH), freq.reshape(1, HY_FH),
      wout16, wout16, deltas)


FFT_UNROLL = 16


def _fft_dims(l):
    n = 2 * l
    n1 = int(round(math.sqrt(n)))
    assert n1 * n1 == n and n1 % 16 == 0, "sequence length must give a square DFT factorisation"
    return n1, n1


def _fft_pitch(n1):
    return 2 * n1 + SUBLANES


def _fft_tables(l):
    n1, n2 = _fft_dims(l)
    n = n1 * n2
    k1 = np.arange(n1)[:, None]
    m1 = np.arange(n1 // 2)[None, :]
    j2 = np.arange(n2)[:, None, None]
    ang = -2.0 * np.pi * (k1 * m1 / n1)[None] - 2.0 * np.pi * (j2 * k1[None] / n)
    gr, gi = np.cos(ang), np.sin(ang)
    g_fwd = np.concatenate([gr, gi], axis=1)
    g_fwd2 = np.concatenate([np.concatenate([gr, -gi], axis=2), np.concatenate([gi, gr], axis=2)], axis=1)
    ang_i = 2.0 * np.pi * (m1.T * k1.T / n1)[None] + 2.0 * np.pi * (j2 * k1.T[None] / n)
    er, ei = np.cos(ang_i) / n, np.sin(ang_i) / n
    g_inv2 = np.concatenate([np.concatenate([er, -ei], axis=2), np.concatenate([ei, er], axis=2)], axis=1)
    a2 = -2.0 * np.pi * np.arange(n2)[:, None] * np.arange(n2)[None, :] / n2
    fr, fi = np.cos(a2), np.sin(a2)
    f2 = np.block([[fr, -fi], [fi, fr]])
    f2_inv = np.block([[fr, fi], [-fi, fr]])
    f2_half = np.stack([np.concatenate([fr, -fi], axis=1), np.concatenate([fi, fr], axis=1)])
    as16 = lambda a: jnp.asarray(a, F32).astype(BF16)
    return dict(g_fwd=as16(g_fwd), g_fwd2=as16(g_fwd2), g_inv2=as16(g_inv2), f2=as16(f2),
                f2_inv=as16(f2_inv), f2_half=as16(f2_half))


def _fft_stage1(x_ref, g_ref, work_ref, n2_lo, cnt, n1, n2, pitch):
    def body(j, carry):
        jj = n2_lo + j
        xs = x_ref[pl.ds(jj, n1 // 2, stride=n2), :]
        r0 = pl.multiple_of(jj * pitch, SUBLANES)
        work_ref[pl.ds(r0, 2 * n1), :] = _bdot(g_ref[j], xs)
        return carry
    lax.fori_loop(0, cnt, body, 0, unroll=FFT_UNROLL)


def _fft_load_k1(work_ref, k1, n1, n2, pitch):
    br = work_ref[pl.ds(k1, n2, stride=pitch), :]
    bi = work_ref[pl.ds(n1 + k1, n2, stride=pitch), :]
    return jnp.concatenate([br, bi], axis=0)


def _hy_spec_kernel(x_ref, g_ref, f2_ref, t_ref, work_ref, *, n1, n2, nc):
    p = pl.program_id(2)
    pitch = _fft_pitch(n1)
    c2 = n2 // nc
    c1 = n1 // nc

    @pl.when(p < nc)
    def _():
        _fft_stage1(x_ref, g_ref, work_ref, p * c2, c2, n1, n2, pitch)

    @pl.when(p >= nc)
    def _():
        def body(j, carry):
            k1 = (p - nc) * c1 + j
            t_ref[j] = _bdot(f2_ref[...], _fft_load_k1(work_ref, k1, n1, n2, pitch))
            return carry
        lax.fori_loop(0, c1, body, 0, unroll=FFT_UNROLL)


def _fft_nc(l):
    return 8 if l >= 8192 else (2 if l >= 2048 else 1)


def _hy_spectrum(sd, tables):
    _, l, e = sd.shape
    n1, n2 = _fft_dims(l)
    nc = _fft_nc(l)
    g_fwd, f2_half = tables['g_fwd'], tables['f2_half']
    pitch = _fft_pitch(n1)
    return pl.pallas_call(
        functools.partial(_hy_spec_kernel, n1=n1, n2=n2, nc=nc),
        grid=(e // LANES, 2, 2 * nc),
        in_specs=[pl.BlockSpec((None, l, LANES), lambda c, j, p: (j, 0, c)),
                  pl.BlockSpec((n2 // nc, 2 * n1, n1 // 2), lambda c, j, p: (jnp.minimum(p, nc - 1), 0, 0)),
                  pl.BlockSpec((None, n2, 2 * n2), lambda c, j, p: (j, 0, 0))],
        out_specs=pl.BlockSpec((n1 // nc, n2, LANES),
                               lambda c, j, p: (jnp.maximum(p - nc, 0), j, c)),
        out_shape=jax.ShapeDtypeStruct((n1, 2 * n2, e), F32),
        scratch_shapes=[pltpu.VMEM((n2 * pitch, LANES), F32)],
        compiler_params=_cparams(("arbitrary", "arbitrary", "arbitrary"), 48),
        name="hy_spectrum",
    )(sd, g_fwd, f2_half)


def _hy_conv_kernel(u_ref, gf_ref, f2_ref, f2i_ref, t_ref, gi_ref, y_ref, work_ref, *, n1, n2, nc):
    p = pl.program_id(2)
    pitch = _fft_pitch(n1)
    c2 = n2 // nc
    c1 = n1 // nc

    @pl.when(p < nc)
    def _():
        def body(j, carry):
            jj = p * c2 + j
            xs = jnp.concatenate([u_ref[0, pl.ds(jj, n1 // 2, stride=n2), :],
                                  u_ref[1, pl.ds(jj, n1 // 2, stride=n2), :]], axis=0)
            r0 = pl.multiple_of(jj * pitch, SUBLANES)
            work_ref[pl.ds(r0, 2 * n1), :] = _bdot(gf_ref[j], xs)
            return carry
        lax.fori_loop(0, c2, body, 0, unroll=FFT_UNROLL)

    @pl.when(jnp.logical_and(p >= nc, p < 2 * nc))
    def _():
        def body(j, carry):
            k1 = (p - nc) * c1 + j
            x = _bdot(f2_ref[...], _fft_load_k1(work_ref, k1, n1, n2, pitch))
            xr, xi = x[:n2], x[n2:]
            tr, ti = t_ref[j, :n2, :], t_ref[j, n2:, :]
            z = jnp.concatenate([xr * tr - xi * ti, xr * ti + xi * tr], axis=0)
            cmat = _bdot(f2i_ref[...], z)
            work_ref[pl.ds(k1, n2, stride=pitch), :] = cmat[:n2]
            work_ref[pl.ds(n1 + k1, n2, stride=pitch), :] = cmat[n2:]
            return carry
        lax.fori_loop(0, c1, body, 0, unroll=FFT_UNROLL)

    @pl.when(p >= 2 * nc)
    def _():
        def body(j, carry):
            jj = (p - 2 * nc) * c2 + j
            r0 = pl.multiple_of(jj * pitch, SUBLANES)
            d = work_ref[pl.ds(r0, 2 * n1), :]
            y = _bdot(gi_ref[j], d)
            y_ref[0, pl.ds(jj, n1 // 2, stride=n2), :] = y[:n1 // 2]
            y_ref[1, pl.ds(jj, n1 // 2, stride=n2), :] = y[n1 // 2:]
            return carry
        lax.fori_loop(0, c2, body, 0, unroll=FFT_UNROLL)


def _hy_conv(u, spec, tables):
    b, l, e = u.shape
    assert b % 2 == 0, "batch rows are transformed in pairs"
    n1, n2 = _fft_dims(l)
    nc = _fft_nc(l)
    pitch = _fft_pitch(n1)
    clip = lambda v: jnp.clip(v, 0, nc - 1)
    pair_spec = lambda **kw: pl.BlockSpec((2, l, LANES), lambda c, bi, p: (bi, 0, c), **kw)
    return pl.pallas_call(
        functools.partial(_hy_conv_kernel, n1=n1, n2=n2, nc=nc),
        grid=(e // LANES, b // 2, 3 * nc),
        in_specs=[pair_spec(),
                  pl.BlockSpec((n2 // nc, 2 * n1, n1), lambda c, bi, p: (clip(p), 0, 0)),
                  pl.BlockSpec((2 * n2, 2 * n2), lambda c, bi, p: (0, 0)),
                  pl.BlockSpec((2 * n2, 2 * n2), lambda c, bi, p: (0, 0)),
                  pl.BlockSpec((n1 // nc, 2 * n2, LANES), lambda c, bi, p: (clip(p - nc), 0, c)),
                  pl.BlockSpec((n2 // nc, n1, 2 * n1), lambda c, bi, p: (clip(p - 2 * nc), 0, 0))],
        out_specs=pair_spec(pipeline_mode=pl.Buffered(1)),
        out_shape=jax.ShapeDtypeStruct((b, l, e), F32),
        scratch_shapes=[pltpu.VMEM((n2 * pitch, LANES), F32)],
        compiler_params=_cparams(("arbitrary", "arbitrary", "arbitrary"), 56),
        name="hy_conv",
    )(u, tables['g_fwd2'], tables['f2'], tables['f2_inv'], spec, tables['g_inv2'])


def _prep_weights(p):
    c16 = lambda a: a.astype(BF16)
    gw = p['lru_gate_w'][0]
    gw = jnp.concatenate([gw[:, 0], gw[:, 1]], axis=-1)
    gb = p['lru_gate_b'][0].reshape(2, 2, LRU_BLOCKS, 1, LRU_BS)
    gb = jnp.concatenate([gb[:, 0], gb[:, 1]], axis=-1)
    lb = p['hg_lb'].astype(F32)
    return dict(
        ada_w=c16(p['ada_w']), hg_w_in=c16(p['hg_w_in'][0]), hg_w_out=c16(p['hg_w_out'][0]),
        hy_w_in=c16(p['hy_w_in'][0]), hy_w_out=c16(p['hy_w_out'][0]), hy_f_wout=c16(p['hy_f_wout'][0]),
        rt_w_in=c16(p['rt_w_in'][0]), rt_w_out=c16(p['rt_w_out'][0]),
        lru_w_in=c16(p['lru_w_in'][0]), lru_w_out=c16(p['lru_w_out'][0]),
        lru_gate_w=c16(gw), lru_gate_b=gb, hg_lb=lb,
        hg_norm_g=jnp.tile(p['hg_norm_g'][0], HG_HEADS))


def _trunk(x, mod, p, w):
    b, l, d = x.shape
    e = E_WIDTH
    x = x.astype(F32)
    zero_bias = lambda n: jnp.zeros((n,), F32)

    def split(layer):
        m = mod[layer]
        return m[:, :d], m[:, d:2 * d], m[:, 2 * d:]

    shift, scale, gate = split(0)
    proj = _in_proj(x, p['norm_g'][0], scale, shift, w['hg_w_in'], zero_bias(5 * e))
    o = _hgrn2_dir(proj, w['hg_lb'], False, None)
    o = _hgrn2_dir(proj, w['hg_lb'], True, o)
    x = _out_proj(_mix_hgrn2, [o, proj, w['hg_norm_g'].reshape(1, e)],
                  lambda tm: [_row_spec(tm, e, 0), _row_spec(tm, e, 4), _vec_spec(e)],
                  w['hg_w_out'], x, gate)

    shift, scale, gate = split(1)
    proj = _in_proj(x, p['norm_g'][1], scale, shift, w['hy_w_in'], p['hy_b_in'][0], PROJ_DTYPE_NARROW)
    u, g1 = _hy_pre(proj, p['hy_conv_w'][0], p['hy_conv_b'][0])
    tables = _fft_tables(l)
    sd = _hy_filters(l, p['hy_f_w1'][0], p['hy_f_b1'][0], p['hy_f_w2'][0], p['hy_f_b2'][0],
                     w['hy_f_wout'], p['hy_f_freq'][0])
    spec = _hy_spectrum(sd, tables)
    yc = _hy_conv(u, spec, tables)
    x = _out_proj(_mix_hyena, [yc, u, g1, p['hy_skip'][0].reshape(1, e)],
                  lambda tm: [_row_spec(tm, e, 0)] * 3 + [_vec_spec(e)],
                  w['hy_w_out'], x, gate)

    shift, scale, gate = split(2)
    proj = _in_proj(x, p['norm_g'][2], scale, shift, w['rt_w_in'], zero_bias(2 * RT_QK + 2 * e),
                    PROJ_DTYPE_NARROW)
    cos, sin = _rope_tables(l)
    o = _ret_dir(proj, cos, sin, False, None)
    o = _ret_dir(proj, cos, sin, True, o)
    x = _out_proj(_mix_retention, [o, proj, p['rt_gn_g'][0].reshape(1, e)],
                  lambda tm: [_row_spec(tm, e, 0), _row_spec(tm, e, 2), _vec_spec(e)],
                  w['rt_w_out'], x, gate)

    shift, scale, gate = split(3)
    proj = _in_proj(x, p['norm_g'][3], scale, shift, w['lru_w_in'], zero_bias(2 * e))
    y = None
    for dirn in range(2):
        y = _lru_dir(proj, p['lru_conv_w'][0], p['lru_conv_b'][0], w['lru_gate_w'][dirn],
                     w['lru_gate_b'][dirn], p['lru_lambda'][0][dirn], dirn == 1, y)
    return _out_proj(_mix_lru, [y, proj], lambda tm: [_row_spec(tm, e, 0), _row_spec(tm, e, 1)],
                     w['lru_w_out'], x, gate, final_g=p['final_g'])


def kernel(x_prompt, x_sample, c_prompt, c_sample, ada_w, ada_b, norm_g, final_g, hg_lb, hg_w_in, hg_norm_g, hg_w_out, hy_w_in, hy_b_in, hy_conv_w, hy_conv_b, hy_f_w1, hy_f_b1, hy_f_w2, hy_f_b2, hy_f_wout, hy_f_freq, hy_skip, hy_w_out, rt_w_in, rt_gn_g, rt_w_out, lru_w_in, lru_conv_w, lru_conv_b, lru_gate_w, lru_gate_b, lru_lambda, lru_w_out):
    p = dict(ada_w=ada_w, ada_b=ada_b, norm_g=norm_g, final_g=final_g, hg_lb=hg_lb, hg_w_in=hg_w_in,
             hg_norm_g=hg_norm_g, hg_w_out=hg_w_out, hy_w_in=hy_w_in, hy_b_in=hy_b_in,
             hy_conv_w=hy_conv_w, hy_conv_b=hy_conv_b, hy_f_w1=hy_f_w1, hy_f_b1=hy_f_b1,
             hy_f_w2=hy_f_w2, hy_f_b2=hy_f_b2, hy_f_wout=hy_f_wout, hy_f_freq=hy_f_freq,
             hy_skip=hy_skip, hy_w_out=hy_w_out, rt_w_in=rt_w_in, rt_gn_g=rt_gn_g, rt_w_out=rt_w_out,
             lru_w_in=lru_w_in, lru_conv_w=lru_conv_w, lru_conv_b=lru_conv_b, lru_gate_w=lru_gate_w,
             lru_gate_b=lru_gate_b, lru_lambda=lru_lambda, lru_w_out=lru_w_out)
    w = _prep_weights(p)
    bp, bs = c_prompt.shape[0], c_sample.shape[0]
    rows = -(-(bp + bs) // SUBLANES) * SUBLANES
    c_all = jnp.concatenate([c_prompt, c_sample, jnp.zeros((rows - bp - bs, D_MODEL), F32)], axis=0)
    mod = _adaln(c_all.astype(F32), w['ada_w'], ada_b)
    y_prompt = _trunk(x_prompt, mod[:, :bp], p, w).astype(x_prompt.dtype)
    y_sample = _trunk(x_sample, mod[:, bp:bp + bs], p, w).astype(x_sample.dtype)
    return (y_prompt, y_sample)
```

```python
import functools
import math

import numpy as np
import jax
import jax.numpy as jnp
from jax import lax
from jax.experimental import pallas as pl
from jax.experimental.pallas import tpu as pltpu

F32 = jnp.float32
BF16 = jnp.bfloat16

D_MODEL = 1024
DEPTH = 4
E_WIDTH = 2 * D_MODEL
NORM_EPS = 1e-6
LANES = 128
SUBLANES = 8
MIB = 1024 * 1024
MIX_DTYPE = BF16
PROJ_DTYPE_NARROW = BF16

HG_CHUNK = 64
HG_DK = 128
HG_HEADS = E_WIDTH // HG_DK

HY_EMB = 33
HY_BANDS = 16
HY_FH = 64
HY_INNER = 2
HY_FAST_DECAY = 0.3
HY_SLOW_DECAY = 1.5
HY_TARGET = 1e-2

RT_HEADS = 4
RT_QK = D_MODEL
RT_DK = RT_QK // RT_HEADS
RT_DV = E_WIDTH // RT_HEADS
RT_ROPE_BASE = 10000.0
RT_CHUNK = 256

LRU_CONV = 4
LRU_BLOCKS = 16
LRU_BS = E_WIDTH // LRU_BLOCKS
LRU_C = 8.0

_NT = (((1,), (1,)), ((), ()))


def _cparams(sem, vmem_mib):
    return pltpu.CompilerParams(dimension_semantics=sem, vmem_limit_bytes=vmem_mib * MIB)


def _bdot(a, b):
    return jnp.dot(a.astype(BF16), b.astype(BF16), preferred_element_type=F32)


def _bdot_nt(a, b):
    return lax.dot_general(a.astype(BF16), b.astype(BF16), _NT, preferred_element_type=F32)


def _sigmoid(x):
    return jax.nn.sigmoid(x)


def _silu(x):
    return x * _sigmoid(x)


def _expm1(x):
    u = jnp.exp(x)
    plain = jnp.logical_or(u == 1.0, x < -0.5)
    small = (u - 1.0) * x / jnp.where(plain, 1.0, jnp.log(u))
    return jnp.where(u == 1.0, x, jnp.where(x < -0.5, u - 1.0, small))


def _adaln_kernel(c_ref, w_ref, b_ref, o_ref):
    cs = _silu(c_ref[...])
    o_ref[...] = _bdot(cs, w_ref[...]) + b_ref[...]


def _adaln(c_all, ada_w16, ada_b):
    bp, d = c_all.shape
    tn = 1024
    return pl.pallas_call(
        _adaln_kernel,
        grid=(DEPTH, 3 * d // tn),
        in_specs=[pl.BlockSpec((bp, d), lambda l, j: (0, 0)),
                  pl.BlockSpec((None, d, tn), lambda l, j: (l, 0, j)),
                  pl.BlockSpec((None, 1, tn), lambda l, j: (l, 0, j))],
        out_specs=pl.BlockSpec((None, bp, tn), lambda l, j: (l, 0, j)),
        out_shape=jax.ShapeDtypeStruct((DEPTH, bp, 3 * d), F32),
        compiler_params=_cparams(("arbitrary", "arbitrary"), 32),
        name="adaln",
    )(c_all, ada_w16, ada_b.reshape(DEPTH, 1, 3 * d))


def _in_proj_kernel(x_ref, g_ref, sc_ref, sh_ref, w_ref, b_ref, o_ref, h_ref, *, tm):
    r0 = pl.multiple_of(pl.program_id(2) * tm, tm)

    @pl.when(pl.program_id(1) == 0)
    def _():
        x = x_ref[...]
        ms = jnp.mean(x * x, axis=-1, keepdims=True)
        h = x * lax.rsqrt(ms + NORM_EPS) * g_ref[...] * (1.0 + sc_ref[...]) + sh_ref[...]
        h_ref[pl.ds(r0, tm), :] = h.astype(BF16)

    o_ref[...] = (jnp.dot(h_ref[pl.ds(r0, tm), :], w_ref[...], preferred_element_type=F32)
                  + b_ref[...]).astype(o_ref.dtype)


IN_PROJ_TM = 1024
IN_PROJ_TN = 1024


def _in_proj(x, norm_g, scale, shift, w16, bias, out_dtype=F32):
    b, l, d = x.shape
    p = w16.shape[1]
    tm = min(l, IN_PROJ_TM)
    tn = IN_PROJ_TN
    ni = l // tm
    x_rows = lambda bi, j, i: (bi, jnp.where(j == 0, i, ni - 1), 0)
    return pl.pallas_call(
        functools.partial(_in_proj_kernel, tm=tm),
        grid=(b, p // tn, ni),
        in_specs=[pl.BlockSpec((None, tm, d), x_rows),
                  pl.BlockSpec((1, d), lambda bi, j, i: (0, 0)),
                  pl.BlockSpec((None, 1, d), lambda bi, j, i: (bi, 0, 0)),
                  pl.BlockSpec((None, 1, d), lambda bi, j, i: (bi, 0, 0)),
                  pl.BlockSpec((d, tn), lambda bi, j, i: (0, j)),
                  pl.BlockSpec((1, tn), lambda bi, j, i: (0, j))],
        out_specs=pl.BlockSpec((None, tm, tn), lambda bi, j, i: (bi, i, j)),
        out_shape=jax.ShapeDtypeStruct((b, l, p), out_dtype),
        scratch_shapes=[pltpu.VMEM((l, d), BF16)],
        compiler_params=_cparams(("arbitrary", "arbitrary", "arbitrary"), 48),
        name="in_proj",
    )(x, norm_g.reshape(1, d), scale.reshape(b, 1, d), shift.reshape(b, 1, d), w16,
      bias.reshape(1, p))


def _ld(ref):
    return ref[...].astype(F32)


def _head_rms(o, width):
    parts = []
    for s in range(0, o.shape[1], width):
        oh = o[:, s:s + width]
        ms = jnp.mean(oh * oh, axis=-1, keepdims=True)
        parts.append(oh * lax.rsqrt(ms + NORM_EPS))
    return jnp.concatenate(parts, axis=1)


def _mix_hgrn2(o_ref, z_ref, g_ref):
    return (_head_rms(_ld(o_ref), HG_DK) * g_ref[...]) * _silu(_ld(z_ref))


def _mix_hyena(yc_ref, u_ref, g1_ref, skip_ref):
    return _ld(g1_ref) * (_ld(yc_ref) + _ld(u_ref) * skip_ref[...])


def _mix_retention(o_ref, z_ref, g_ref):
    return (_head_rms(_ld(o_ref), RT_DV) * g_ref[...]) * _silu(_ld(z_ref))


def _mix_lru(y_ref, z_ref):
    return _ld(y_ref) * _silu(_ld(z_ref))


def _out_proj_kernel(*refs, mix, n_mix, final):
    mix_refs = refs[:n_mix]
    w_ref, x_ref, gate_ref = refs[n_mix:n_mix + 3]
    o_ref = refs[-1]
    y = mix(*mix_refs)
    out = x_ref[...] + gate_ref[...] * _bdot(y, w_ref[...])
    if final:
        fg_ref = refs[n_mix + 3]
        ms = jnp.mean(out * out, axis=-1, keepdims=True)
        out = out * lax.rsqrt(ms + NORM_EPS) * fg_ref[...]
    o_ref[...] = out


def _out_proj(mix, mix_args, mix_specs, w16, x, gate, final_g=None):
    b, l, d = x.shape
    e = w16.shape[0]
    tm = min(l, 256)
    in_specs = list(mix_specs(tm)) + [
        pl.BlockSpec((e, d), lambda bi, i: (0, 0)),
        pl.BlockSpec((None, tm, d), lambda bi, i: (bi, i, 0)),
        pl.BlockSpec((None, 1, d), lambda bi, i: (bi, 0, 0))]
    args = list(mix_args) + [w16, x, gate.reshape(b, 1, d)]
    if final_g is not None:
        in_specs.append(pl.BlockSpec((1, d), lambda bi, i: (0, 0)))
        args.append(final_g.reshape(1, d))
    return pl.pallas_call(
        functools.partial(_out_proj_kernel, mix=mix, n_mix=len(mix_args), final=final_g is not None),
        grid=(b, l // tm),
        in_specs=in_specs,
        out_specs=pl.BlockSpec((None, tm, d), lambda bi, i: (bi, i, 0)),
        out_shape=jax.ShapeDtypeStruct((b, l, d), F32),
        compiler_params=_cparams(("arbitrary", "arbitrary"), 48),
        name="out_proj",
    )(*args)


def _row_spec(tm, width, col):
    return pl.BlockSpec((None, tm, width), lambda bi, i: (bi, i, col))


def _vec_spec(width):
    return pl.BlockSpec((1, width), lambda bi, i: (0, 0))


def _cumsum_rows(x, reverse):
    sub = lax.broadcasted_iota(jnp.int32, (SUBLANES, x.shape[1]), 0)
    groups = x.shape[0] // SUBLANES
    order = range(groups - 1, -1, -1) if reverse else range(groups)
    edge = 0 if reverse else SUBLANES - 1
    out = [None] * groups
    total = None
    for gi in order:
        xg = x[gi * SUBLANES:(gi + 1) * SUBLANES, :]
        s = 1
        while s < SUBLANES:
            valid = (sub < SUBLANES - s) if reverse else (sub >= s)
            xg = xg + jnp.where(valid, pltpu.roll(xg, (SUBLANES - s) if reverse else s, 0), 0.0)
            s *= 2
        if total is not None:
            xg = xg + total
        total = xg[edge:edge + 1, :]
        out[gi] = xg
    return jnp.concatenate(out, axis=0)


def _hgrn2_kernel(*refs, reverse, hb, nch, add):
    q_ref, f_ref, v_ref, lb_ref = refs[:4]
    prev_ref = refs[4] if add else None
    o_ref, st_ref = refs[-2:]

    @pl.when(pl.program_id(2) == 0)
    def _():
        st_ref[...] = jnp.zeros_like(st_ref)

    lb_exp = jnp.exp(lb_ref[...] - jnp.max(lb_ref[...], axis=0, keepdims=True))
    lb_all = lb_exp[0:1, :] / jnp.sum(lb_exp, axis=0, keepdims=True)

    c = HG_CHUNK
    row = lax.broadcasted_iota(jnp.int32, (c, c), 0)
    col = lax.broadcasted_iota(jnp.int32, (c, c), 1)
    mask = (col >= row) if reverse else (col <= row)
    mid = c // 2
    ref_row = (c - 1 - mid) if reverse else mid
    last_row = 0 if reverse else c - 1

    def chunk(ci, carry):
        cc = (nch - 1 - ci) if reverse else ci
        r0 = pl.multiple_of(cc * c, c)
        for hh in range(hb):
            sl = slice(hh * HG_DK, (hh + 1) * HG_DK)
            q = _silu(q_ref[pl.ds(r0, c), sl])
            lb = lb_all[:, sl]
            f = lb + (1.0 - lb) * _sigmoid(f_ref[pl.ds(r0, c), sl])
            k = 1.0 - f
            g = jnp.log(f)
            v = v_ref[pl.ds(r0, c), sl]
            bsum = _cumsum_rows(g, reverse)
            b_ref_row = bsum[ref_row:ref_row + 1, :]
            b_last = bsum[last_row:last_row + 1, :]
            scores = _bdot_nt(q * jnp.exp(bsum - b_ref_row), k * jnp.exp(b_ref_row - bsum))
            scores = jnp.where(mask, scores, 0.0)
            st = st_ref[hh]
            vt = v.T
            o = _bdot_nt(jnp.concatenate([q * jnp.exp(bsum), scores], axis=1),
                         jnp.concatenate([st, vt], axis=1))
            st_ref[hh] = st * jnp.exp(b_last) + _bdot(vt, k * jnp.exp(b_last - bsum))
            if add:
                o = o + prev_ref[pl.ds(r0, c), sl].astype(F32)
            o_ref[pl.ds(r0, c), sl] = o.astype(o_ref.dtype)
        return carry

    lax.fori_loop(0, nch, chunk, 0, unroll=2)


def _hgrn2_dir(proj, lb, reverse, prev):
    b, l, _ = proj.shape
    e = E_WIDTH
    hb = 8
    w = hb * HG_DK
    t = min(l, 512)
    nt = l // t
    ncol = e // w
    fsec = 2 if reverse else 1

    def rows(bi, h, ti):
        return (nt - 1 - ti) if reverse else ti

    def sec(s):
        return pl.BlockSpec((None, t, w), lambda bi, h, ti: (bi, rows(bi, h, ti), s * ncol + h))

    in_specs = [sec(0), sec(fsec), sec(3), pl.BlockSpec((DEPTH + 1, w), lambda bi, h, ti: (0, h))]
    args = [proj, proj, proj, lb]
    out_spec = pl.BlockSpec((None, t, w), lambda bi, h, ti: (bi, rows(bi, h, ti), h))
    if prev is not None:
        in_specs.append(out_spec)
        args.append(prev)
    return pl.pallas_call(
        functools.partial(_hgrn2_kernel, reverse=reverse, hb=hb, nch=t // HG_CHUNK,
                          add=prev is not None),
        grid=(b, ncol, nt),
        in_specs=in_specs,
        out_specs=out_spec,
        out_shape=jax.ShapeDtypeStruct((b, l, e), MIX_DTYPE),
        scratch_shapes=[pltpu.VMEM((hb, HG_DK, HG_DK), F32)],
        compiler_params=_cparams(("arbitrary", "arbitrary", "arbitrary"), 32),
        name="hgrn2_bwd" if reverse else "hgrn2_fwd",
    )(*args)


def _ret_tables(reverse):
    c = RT_CHUNK
    hidx = np.arange(RT_HEADS, dtype=np.float64)
    lg = np.log1p(-np.exp2((-5.5 if reverse else -5.0) - hidx))[:, None]
    pos = np.arange(c, dtype=np.float64)[None, :]
    rel = pos[0][:, None] - pos[0][None, :]
    if reverse:
        rel = -rel
    decay = np.where(rel >= 0, np.exp(lg[:, :, None] * np.maximum(rel, 0.0)[None]), 0.0)
    q_dec = np.exp(lg * ((c - pos) if reverse else (pos + 1.0)))
    k_dec = np.exp(lg * (pos if reverse else (c - 1.0 - pos)))
    c_dec = np.exp(lg * c)
    return (jnp.asarray(decay, F32),
            jnp.asarray(np.broadcast_to(q_dec[:, :, None], (RT_HEADS, c, RT_DV)), F32),
            jnp.asarray(np.broadcast_to(k_dec[:, :, None], (RT_HEADS, c, RT_DK)), F32),
            jnp.asarray(np.broadcast_to(c_dec[:, :, None], (RT_HEADS, 1, RT_DV)), F32))


def _rope_tables(l):
    inv = RT_ROPE_BASE ** (-jnp.arange(0, RT_DK, 2, dtype=F32) / RT_DK)
    ang = jnp.arange(l, dtype=F32)[:, None] * inv[None]
    return jnp.cos(ang), jnp.sin(ang)


def _ret_kernel(*refs, add):
    q_ref, k_ref, v_ref, cos_ref, sin_ref, dec_ref, qd_ref, kd_ref, cd_ref = refs[:9]
    prev_ref = refs[9] if add else None
    o_ref, r_ref = refs[-2:]

    @pl.when(pl.program_id(1) == 0)
    def _():
        r_ref[...] = jnp.zeros_like(r_ref)

    cos = cos_ref[...]
    sin = sin_ref[...]
    half = RT_DK // 2

    def rot(t):
        t1 = t[:, :half]
        t2 = t[:, half:]
        return jnp.concatenate([t1 * cos - t2 * sin, t1 * sin + t2 * cos], axis=1)

    for h in range(RT_HEADS):
        qs = slice(h * RT_DK, (h + 1) * RT_DK)
        vs = slice(h * RT_DV, (h + 1) * RT_DV)
        q = rot(q_ref[:, qs].astype(F32))
        k = rot(k_ref[:, qs].astype(F32)) * (RT_DK ** -0.5)
        v = v_ref[:, vs]
        scores = _bdot_nt(q, k) * dec_ref[h]
        r = r_ref[h]
        o = _bdot(scores, v) + qd_ref[h] * _bdot(q, r)
        r_ref[h] = cd_ref[h] * r + _bdot((k * kd_ref[h]).T, v)
        if add:
            o = o + prev_ref[:, vs].astype(F32)
        o_ref[:, vs] = o.astype(o_ref.dtype)


def _ret_dir(proj, cos, sin, reverse, prev):
    b, l, _ = proj.shape
    c = RT_CHUNK
    nt = l // c
    e = E_WIDTH
    dec, qd, kd, cd = _ret_tables(reverse)

    def rows(ti):
        return (nt - 1 - ti) if reverse else ti

    whole = lambda a: pl.BlockSpec(a.shape, lambda bi, ti: (0, 0, 0))
    in_specs = [
        pl.BlockSpec((None, c, RT_QK), lambda bi, ti: (bi, rows(ti), 0)),
        pl.BlockSpec((None, c, RT_QK), lambda bi, ti: (bi, rows(ti), 1)),
        pl.BlockSpec((None, c, e), lambda bi, ti: (bi, rows(ti), 2 * RT_QK // e)),
        pl.BlockSpec((c, RT_DK // 2), lambda bi, ti: (rows(ti), 0)),
        pl.BlockSpec((c, RT_DK // 2), lambda bi, ti: (rows(ti), 0)),
        whole(dec), whole(qd), whole(kd), whole(cd),
    ]
    args = [proj, proj, proj, cos, sin, dec, qd, kd, cd]
    out_spec = pl.BlockSpec((None, c, e), lambda bi, ti: (bi, rows(ti), 0))
    if prev is not None:
        in_specs.append(out_spec)
        args.append(prev)
    return pl.pallas_call(
        functools.partial(_ret_kernel, add=prev is not None),
        grid=(b, nt),
        in_specs=in_specs,
        out_specs=out_spec,
        out_shape=jax.ShapeDtypeStruct((b, l, e), MIX_DTYPE),
        scratch_shapes=[pltpu.VMEM((RT_HEADS, RT_DK, RT_DV), F32)],
        compiler_params=_cparams(("arbitrary", "arbitrary"), 48),
        name="ret_bwd" if reverse else "ret_fwd",
    )(*args)


def _halo_rows(dtype):
    return SUBLANES * (4 // jnp.dtype(dtype).itemsize)


def _halo_specs(t, w, l, col, order, hr=SUBLANES):
    per = t // hr
    nblk = l // hr
    prev = pl.BlockSpec((None, hr, w),
                        lambda *g: (g[0], jnp.maximum(order(*g) * per - 1, 0), col(*g)))
    nxt = pl.BlockSpec((None, hr, w),
                       lambda *g: (g[0], jnp.minimum((order(*g) + 1) * per, nblk - 1), col(*g)))
    return prev, nxt


def _fill_ext(ext_ref, x_ref, xp_ref, xn_ref, first, last, t):
    hr = xp_ref.shape[0]
    ext_ref[0:SUBLANES, :] = jnp.where(first, 0.0, xp_ref[hr - SUBLANES:hr, :].astype(F32))
    ext_ref[SUBLANES:SUBLANES + t, :] = x_ref[...].astype(F32)
    ext_ref[SUBLANES + t:2 * SUBLANES + t, :] = jnp.where(last, 0.0, xn_ref[0:SUBLANES, :].astype(F32))


def _lru_pitches(t):
    seg = t // SUBLANES
    return seg, seg + 3 * SUBLANES, seg + SUBLANES


def _lru_kernel(*refs, reverse, t, nt, add):
    x_ref, xp_ref, xn_ref, cw_ref, cb_ref, gw_ref, gb_ref, lam_ref = refs[:8]
    prev_ref = refs[8] if add else None
    o_ref, ext_ref, hbuf_ref, carry_ref = refs[-4:]
    ti = pl.program_id(1)
    te = (nt - 1 - ti) if reverse else ti
    seg, pin, pout = _lru_pitches(t)
    nseg = SUBLANES
    left = LRU_CONV // 2

    @pl.when(ti == 0)
    def _():
        carry_ref[...] = jnp.zeros_like(carry_ref)

    neg_lam = -lam_ref[...]
    softplus = jnp.maximum(neg_lam, 0.0) + jnp.log1p(jnp.exp(-jnp.abs(neg_lam)))
    jorder = range(seg - 1, -1, -1) if reverse else range(seg)
    sorder = range(nseg - 1, -1, -1) if reverse else range(nseg)
    edge = 0 if reverse else seg - 1

    for n in range(LRU_BLOCKS):
        sl = slice(n * LRU_BS, (n + 1) * LRU_BS)
        for s in range(nseg):
            r0 = s * seg
            before = (jnp.where(te == 0, 0.0, xp_ref[:, sl]) if s == 0
                      else x_ref[r0 - SUBLANES:r0, sl])
            after = (jnp.where(te == nt - 1, 0.0, xn_ref[:, sl]) if s == nseg - 1
                     else x_ref[r0 + seg:r0 + seg + SUBLANES, sl])
            ext_ref[n, s * pin:s * pin + SUBLANES, :] = before
            ext_ref[n, s * pin + SUBLANES:s * pin + SUBLANES + seg, :] = x_ref[r0:r0 + seg, sl]
            ext_ref[n, s * pin + SUBLANES + seg:s * pin + 2 * SUBLANES + seg, :] = after
        xs = []
        for j in range(seg):
            acc = cb_ref[:, sl]
            for d in range(LRU_CONV):
                acc = acc + cw_ref[d:d + 1, sl] * ext_ref[n, pl.ds(SUBLANES + j + d - left, nseg, stride=pin), :]
            xs.append(acc)
        xn = jnp.concatenate(xs, axis=0)
        gates = _bdot(xn, gw_ref[n]) + gb_ref[n]
        r = _sigmoid(gates[:, :LRU_BS])
        i = _sigmoid(gates[:, LRU_BS:])
        log_a = -LRU_C * r * softplus[:, sl]
        a = jnp.exp(log_a)
        bb = jnp.sqrt(-_expm1(2.0 * log_a)) * (i * xn)
        hs = [None] * seg
        ps = [None] * seg
        h = p = None
        for j in jorder:
            aj = a[j * nseg:(j + 1) * nseg, :]
            bj = bb[j * nseg:(j + 1) * nseg, :]
            h = bj if h is None else aj * h + bj
            p = aj if p is None else aj * p
            hs[j], ps[j] = h, p
        c = carry_ref[:, sl]
        enter = [None] * nseg
        for s in sorder:
            enter[s] = c
            c = hs[edge][s:s + 1, :] + ps[edge][s:s + 1, :] * c
        carry_ref[:, sl] = c
        cin = jnp.concatenate(enter, axis=0)
        for j in range(seg):
            hbuf_ref[n, pl.ds(j, nseg, stride=pout), :] = hs[j] + ps[j] * cin
        for s in range(nseg):
            hn = hbuf_ref[n, s * pout:s * pout + seg, :]
            if add:
                hn = hn + prev_ref[s * seg:(s + 1) * seg, sl].astype(F32)
            o_ref[s * seg:(s + 1) * seg, sl] = hn.astype(o_ref.dtype)


def _lru_dir(proj, conv_w, conv_b, gate_w16, gate_b, lam, reverse, prev):
    b, l, _ = proj.shape
    e = E_WIDTH
    t = min(l, 256)
    nt = l // t

    def order(bi, ti):
        return (nt - 1 - ti) if reverse else ti

    xp_spec, xn_spec = _halo_specs(t, e, l, lambda bi, ti: 0, order)
    in_specs = [
        pl.BlockSpec((None, t, e), lambda bi, ti: (bi, order(bi, ti), 0)), xp_spec, xn_spec,
        pl.BlockSpec((LRU_CONV, e), lambda bi, ti: (0, 0)),
        pl.BlockSpec((1, e), lambda bi, ti: (0, 0)),
        pl.BlockSpec((LRU_BLOCKS, LRU_BS, 2 * LRU_BS), lambda bi, ti: (0, 0, 0)),
        pl.BlockSpec((LRU_BLOCKS, 1, 2 * LRU_BS), lambda bi, ti: (0, 0, 0)),
        pl.BlockSpec((1, e), lambda bi, ti: (0, 0)),
    ]
    args = [proj, proj, proj, conv_w, conv_b.reshape(1, e), gate_w16, gate_b, lam.reshape(1, e)]
    out_spec = pl.BlockSpec((None, t, e), lambda bi, ti: (bi, order(bi, ti), 0))
    if prev is not None:
        in_specs.append(out_spec)
        args.append(prev)
    return pl.pallas_call(
        functools.partial(_lru_kernel, reverse=reverse, t=t, nt=nt, add=prev is not None),
        grid=(b, nt),
        in_specs=in_specs,
        out_specs=out_spec,
        out_shape=jax.ShapeDtypeStruct((b, l, e), MIX_DTYPE),
        scratch_shapes=[pltpu.VMEM((LRU_BLOCKS, SUBLANES * _lru_pitches(t)[1], LRU_BS), F32),
                        pltpu.VMEM((LRU_BLOCKS, SUBLANES * _lru_pitches(t)[2], LRU_BS), F32),
                        pltpu.VMEM((1, e), F32)],
        compiler_params=_cparams(("arbitrary", "arbitrary"), 48),
        name="lru_bwd" if reverse else "lru_fwd",
    )(*args)


def _hy_pre_kernel(x0_ref, x0p_ref, x0n_ref, x1_ref, x1p_ref, x1n_ref, v_ref, vp_ref, vn_ref,
                   z_ref, w0_ref, w1_ref, wv_ref, b0_ref, b1_ref, bv_ref,
                   u_ref, g1_ref, e0_ref, e1_ref, ev_ref, *, t, nt):
    ti = pl.program_id(1)
    first = ti == 0
    last = ti == nt - 1

    def conv(x_ref, xp_ref, xn_ref, ext_ref, w_ref, b_ref):
        _fill_ext(ext_ref, x_ref, xp_ref, xn_ref, first, last, t)
        out = b_ref[...]
        for j in range(3):
            out = out + w_ref[j:j + 1, :] * ext_ref[pl.ds(SUBLANES - 1 + j, t), :]
        return out

    x0 = conv(x0_ref, x0p_ref, x0n_ref, e0_ref, w0_ref, b0_ref)
    x1 = conv(x1_ref, x1p_ref, x1n_ref, e1_ref, w1_ref, b1_ref)
    v = conv(v_ref, vp_ref, vn_ref, ev_ref, wv_ref, bv_ref)
    u_ref[...] = x0 * v
    g1_ref[...] = (x1 * _silu(z_ref[...].astype(F32))).astype(g1_ref.dtype)


def _hy_pre(proj, conv_w, conv_b):
    b, l, _ = proj.shape
    e = E_WIDTH
    w = 512
    t = min(l, 512)
    nt = l // t
    ncol = e // w

    def order(bi, ti, j):
        return ti

    specs, args = [], []
    for s in range(3):
        col = (lambda s: lambda bi, ti, j: s * ncol + j)(s)
        xp, xn = _halo_specs(t, w, l, col, order, _halo_rows(proj.dtype))
        specs += [pl.BlockSpec((None, t, w), (lambda col: lambda bi, ti, j: (bi, ti, col(bi, ti, j)))(col)),
                  xp, xn]
        args += [proj, proj, proj]
    specs.append(pl.BlockSpec((None, t, w), lambda bi, ti, j: (bi, ti, 3 * ncol + j)))
    args.append(proj)
    for s in range(3):
        specs.append(pl.BlockSpec((3, w), (lambda s: lambda bi, ti, j: (0, s * ncol + j))(s)))
        args.append(conv_w)
    cb = conv_b.reshape(1, 3 * e)
    for s in range(3):
        specs.append(pl.BlockSpec((1, w), (lambda s: lambda bi, ti, j: (0, s * ncol + j))(s)))
        args.append(cb)
    out_spec = pl.BlockSpec((None, t, w), lambda bi, ti, j: (bi, ti, j))
    return pl.pallas_call(
        functools.partial(_hy_pre_kernel, t=t, nt=nt),
        grid=(b, nt, ncol),
        in_specs=specs,
        out_specs=[out_spec, out_spec],
        out_shape=[jax.ShapeDtypeStruct((b, l, e), F32), jax.ShapeDtypeStruct((b, l, e), MIX_DTYPE)],
        scratch_shapes=[pltpu.VMEM((t + 2 * SUBLANES, w), F32)] * 3,
        compiler_params=_cparams(("arbitrary", "arbitrary", "arbitrary"), 48),
        name="hy_pre",
    )(*args)


def _hy_filter_kernel(z_ref, w1_ref, b1_ref, w2_ref, b2_ref, fr_ref, wf_ref, wb_ref, dl_ref,
                      sd_ref, a_ref):
    @pl.when(pl.program_id(1) == 0)
    def _():
        fr = fr_ref[...]
        a = jnp.sin(fr * (_bdot(z_ref[...], w1_ref[...]) + b1_ref[...]))
        for j in range(HY_INNER):
            a = jnp.sin(fr * (_bdot(a, w2_ref[j]) + b2_ref[j]))
        a_ref[...] = a

    a = a_ref[...]
    window = jnp.exp(-z_ref[:, 0:1] * dl_ref[...])
    h_fw = _bdot(a, wf_ref[...]) * window
    h_bw = _bdot(a, wb_ref[...]) * window
    sd_ref[0] = h_fw + h_bw
    sd_ref[1] = h_fw - h_bw


def _hy_filters(l, w1, b1, w2, b2, wout16, freq):
    e = E_WIDTH
    kp = LANES
    t = jnp.linspace(0.0, 1.0, l, dtype=F32)[:, None]
    wv = 2.0 * math.pi * jnp.arange(l, dtype=F32)[:, None] / l
    bands = jnp.linspace(1e-4, HY_BANDS - 1, HY_BANDS, dtype=F32)[None]
    z = jnp.concatenate([t, jnp.cos(bands * wv), -jnp.sin(bands * wv),
                         jnp.zeros((l, kp - HY_EMB), F32)], axis=-1)
    w1p = jnp.concatenate([w1, jnp.zeros((kp - HY_EMB, HY_FH), F32)], axis=0)
    max_decay = math.log(HY_TARGET) / HY_FAST_DECAY
    min_decay = math.log(HY_TARGET) / HY_SLOW_DECAY
    deltas = jnp.abs(jnp.linspace(min_decay, max_decay, e, dtype=F32))[None]
    tm = min(l, 512)
    w = 512
    ncol = e // w
    return pl.pallas_call(
        _hy_filter_kernel,
        grid=(l // tm, ncol),
        in_specs=[pl.BlockSpec((tm, kp), lambda i, j: (i, 0)),
                  pl.BlockSpec((kp, HY_FH), lambda i, j: (0, 0)),
                  pl.BlockSpec((1, HY_FH), lambda i, j: (0, 0)),
                  pl.BlockSpec((HY_INNER, HY_FH, HY_FH), lambda i, j: (0, 0, 0)),
                  pl.BlockSpec((HY_INNER, 1, HY_FH), lambda i, j: (0, 0, 0)),
                  pl.BlockSpec((1, HY_FH), lambda i, j: (0, 0)),
                  pl.BlockSpec((HY_FH, w), lambda i, j: (0, j)),
                  pl.BlockSpec((HY_FH, w), lambda i, j: (0, ncol + j)),
                  pl.BlockSpec((1, w), lambda i, j: (0, j))],
        out_specs=pl.BlockSpec((2, tm, w), lambda i, j: (0, i, j)),
        out_shape=jax.ShapeDtypeStruct((2, l, e), F32),
        scratch_shapes=[pltpu.VMEM((tm, HY_FH), F32)],
        compiler_params=_cparams(("arbitrary", "arbitrary"), 32),
        name="hy_filter",
    )(z, w1p, b1.reshape(1, HY_FH), w2, b2.reshape(HY_INNER, 1, HY_FH), freq.reshape(1, HY_FH),
      wout16, wout16, deltas)


FFT_UNROLL = 16


def _fft_dims(l):
    n = 2 * l
    n1 = int(round(math.sqrt(n)))
    assert n1 * n1 == n and n1 % 16 == 0, "sequence length must give a square DFT factorisation"
    return n1, n1


def _fft_pitch(n1):
    return 2 * n1 + SUBLANES


def _fft_tables(l):
    n1, n2 = _fft_dims(l)
    n = n1 * n2
    k1 = np.arange(n1)[:, None]
    m1 = np.arange(n1 // 2)[None, :]
    j2 = np.arange(n2)[:, None, None]
    ang = -2.0 * np.pi * (k1 * m1 / n1)[None] - 2.0 * np.pi * (j2 * k1[None] / n)
    gr, gi = np.cos(ang), np.sin(ang)
    g_fwd = np.concatenate([gr, gi], axis=1)
    g_fwd2 = np.concatenate([np.concatenate([gr, -gi], axis=2), np.concatenate([gi, gr], axis=2)], axis=1)
    ang_i = 2.0 * np.pi * (m1.T * k1.T / n1)[None] + 2.0 * np.pi * (j2 * k1.T[None] / n)
    er, ei = np.cos(ang_i) / n, np.sin(ang_i) / n
    g_inv2 = np.concatenate([np.concatenate([er, -ei], axis=2), np.concatenate([ei, er], axis=2)], axis=1)
    a2 = -2.0 * np.pi * np.arange(n2)[:, None] * np.arange(n2)[None, :] / n2
    fr, fi = np.cos(a2), np.sin(a2)
    f2 = np.block([[fr, -fi], [fi, fr]])
    f2_inv = np.block([[fr, fi], [-fi, fr]])
    f2_half = np.stack([np.concatenate([fr, -fi], axis=1), np.concatenate([fi, fr], axis=1)])
    as16 = lambda a: jnp.asarray(a, F32).astype(BF16)
    return dict(g_fwd=as16(g_fwd), g_fwd2=as16(g_fwd2), g_inv2=as16(g_inv2), f2=as16(f2),
                f2_inv=as16(f2_inv), f2_half=as16(f2_half))


def _fft_stage1(x_ref, g_ref, work_ref, n2_lo, cnt, n1, n2, pitch):
    def body(j, carry):
        jj = n2_lo + j
        xs = x_ref[pl.ds(jj, n1 // 2, stride=n2), :]
        r0 = pl.multiple_of(jj * pitch, SUBLANES)
        work_ref[pl.ds(r0, 2 * n1), :] = _bdot(g_ref[j], xs)
        return carry
    lax.fori_loop(0, cnt, body, 0, unroll=FFT_UNROLL)


def _fft_load_k1(work_ref, k1, n1, n2, pitch):
    br = work_ref[pl.ds(k1, n2, stride=pitch), :]
    bi = work_ref[pl.ds(n1 + k1, n2, stride=pitch), :]
    return jnp.concatenate([br, bi], axis=0)


def _hy_spec_kernel(x_ref, g_ref, f2_ref, t_ref, work_ref, *, n1, n2, nc):
    p = pl.program_id(2)
    pitch = _fft_pitch(n1)
    c2 = n2 // nc
    c1 = n1 // nc

    @pl.when(p < nc)
    def _():
        _fft_stage1(x_ref, g_ref, work_ref, p * c2, c2, n1, n2, pitch)

    @pl.when(p >= nc)
    def _():
        def body(j, carry):
            k1 = (p - nc) * c1 + j
            t_ref[j] = _bdot(f2_ref[...], _fft_load_k1(work_ref, k1, n1, n2, pitch))
            return carry
        lax.fori_loop(0, c1, body, 0, unroll=FFT_UNROLL)


def _fft_nc(l):
    return 8 if l >= 8192 else (2 if l >= 2048 else 1)


def _hy_spectrum(sd, tables):
    _, l, e = sd.shape
    n1, n2 = _fft_dims(l)
    nc = _fft_nc(l)
    g_fwd, f2_half = tables['g_fwd'], tables['f2_half']
    pitch = _fft_pitch(n1)
    return pl.pallas_call(
        functools.partial(_hy_spec_kernel, n1=n1, n2=n2, nc=nc),
        grid=(e // LANES, 2, 2 * nc),
        in_specs=[pl.BlockSpec((None, l, LANES), lambda c, j, p: (j, 0, c)),
                  pl.BlockSpec((n2 // nc, 2 * n1, n1 // 2), lambda c, j, p: (jnp.minimum(p, nc - 1), 0, 0)),
                  pl.BlockSpec((None, n2, 2 * n2), lambda c, j, p: (j, 0, 0))],
        out_specs=pl.BlockSpec((n1 // nc, n2, LANES),
                               lambda c, j, p: (jnp.maximum(p - nc, 0), j, c)),
        out_shape=jax.ShapeDtypeStruct((n1, 2 * n2, e), F32),
        scratch_shapes=[pltpu.VMEM((n2 * pitch, LANES), F32)],
        compiler_params=_cparams(("arbitrary", "arbitrary", "arbitrary"), 48),
        name="hy_spectrum",
    )(sd, g_fwd, f2_half)


def _hy_conv_kernel(u_ref, gf_ref, f2_ref, f2i_ref, t_ref, gi_ref, y_ref, work_ref, *, n1, n2, nc):
    p = pl.program_id(2)
    pitch = _fft_pitch(n1)
    c2 = n2 // nc
    c1 = n1 // nc

    @pl.when(p < nc)
    def _():
        def body(j, carry):
            jj = p * c2 + j
            xs = jnp.concatenate([u_ref[0, pl.ds(jj, n1 // 2, stride=n2), :],
                                  u_ref[1, pl.ds(jj, n1 // 2, stride=n2), :]], axis=0)
            r0 = pl.multiple_of(jj * pitch, SUBLANES)
            work_ref[pl.ds(r0, 2 * n1), :] = _bdot(gf_ref[j], xs)
            return carry
        lax.fori_loop(0, c2, body, 0, unroll=FFT_UNROLL)

    @pl.when(jnp.logical_and(p >= nc, p < 2 * nc))
    def _():
        def body(j, carry):
            k1 = (p - nc) * c1 + j
            x = _bdot(f2_ref[...], _fft_load_k1(work_ref, k1, n1, n2, pitch))
            xr, xi = x[:n2], x[n2:]
            tr, ti = t_ref[j, :n2, :], t_ref[j, n2:, :]
            z = jnp.concatenate([xr * tr - xi * ti, xr * ti + xi * tr], axis=0)
            cmat = _bdot(f2i_ref[...], z)
            work_ref[pl.ds(k1, n2, stride=pitch), :] = cmat[:n2]
            work_ref[pl.ds(n1 + k1, n2, stride=pitch), :] = cmat[n2:]
            return carry
        lax.fori_loop(0, c1, body, 0, unroll=FFT_UNROLL)

    @pl.when(p >= 2 * nc)
    def _():
        def body(j, carry):
            jj = (p - 2 * nc) * c2 + j
            r0 = pl.multiple_of(jj * pitch, SUBLANES)
            d = work_ref[pl.ds(r0, 2 * n1), :]
            y = _bdot(gi_ref[j], d)
            y_ref[0, pl.ds(jj, n1 // 2, stride=n2), :] = y[:n1 // 2]
            y_ref[1, pl.ds(jj, n1 // 2, stride=n2), :] = y[n1 // 2:]
            return carry
        lax.fori_loop(0, c2, body, 0, unroll=FFT_UNROLL)


def _hy_conv(u, spec, tables):
    b, l, e = u.shape
    assert b % 2 == 0, "batch rows are transformed in pairs"
    n1, n2 = _fft_dims(l)
    nc = _fft_nc(l)
    pitch = _fft_pitch(n1)
    clip = lambda v: jnp.clip(v, 0, nc - 1)
    pair_spec = lambda **kw: pl.BlockSpec((2, l, LANES), lambda c, bi, p: (bi, 0, c), **kw)
    return pl.pallas_call(
        functools.partial(_hy_conv_kernel, n1=n1, n2=n2, nc=nc),
        grid=(e // LANES, b // 2, 3 * nc),
        in_specs=[pair_spec(),
                  pl.BlockSpec((n2 // nc, 2 * n1, n1), lambda c, bi, p: (clip(p), 0, 0)),
                  pl.BlockSpec((2 * n2, 2 * n2), lambda c, bi, p: (0, 0)),
                  pl.BlockSpec((2 * n2, 2 * n2), lambda c, bi, p: (0, 0)),
                  pl.BlockSpec((n1 // nc, 2 * n2, LANES), lambda c, bi, p: (clip(p - nc), 0, c)),
                  pl.BlockSpec((n2 // nc, n1, 2 * n1), lambda c, bi, p: (clip(p - 2 * nc), 0, 0))],
        out_specs=pair_spec(pipeline_mode=pl.Buffered(1)),
        out_shape=jax.ShapeDtypeStruct((b, l, e), F32),
        scratch_shapes=[pltpu.VMEM((n2 * pitch, LANES), F32)],
        compiler_params=_cparams(("arbitrary", "arbitrary", "arbitrary"), 56),
        name="hy_conv",
    )(u, tables['g_fwd2'], tables['f2'], tables['f2_inv'], spec, tables['g_inv2'])


def _prep_weights(p):
    c16 = lambda a: a.astype(BF16)
    gw = p['lru_gate_w'][0]
    gw = jnp.concatenate([gw[:, 0], gw[:, 1]], axis=-1)
    gb = p['lru_gate_b'][0].reshape(2, 2, LRU_BLOCKS, 1, LRU_BS)
    gb = jnp.concatenate([gb[:, 0], gb[:, 1]], axis=-1)
    lb = p['hg_lb'].astype(F32)
    return dict(
        ada_w=c16(p['ada_w']), hg_w_in=c16(p['hg_w_in'][0]), hg_w_out=c16(p['hg_w_out'][0]),
        hy_w_in=c16(p['hy_w_in'][0]), hy_w_out=c16(p['hy_w_out'][0]), hy_f_wout=c16(p['hy_f_wout'][0]),
        rt_w_in=c16(p['rt_w_in'][0]), rt_w_out=c16(p['rt_w_out'][0]),
        lru_w_in=c16(p['lru_w_in'][0]), lru_w_out=c16(p['lru_w_out'][0]),
        lru_gate_w=c16(gw), lru_gate_b=gb, hg_lb=lb,
        hg_norm_g=jnp.tile(p['hg_norm_g'][0], HG_HEADS))


def _trunk(x, mod, p, w):
    b, l, d = x.shape
    e = E_WIDTH
    x = x.astype(F32)
    zero_bias = lambda n: jnp.zeros((n,), F32)

    def split(layer):
        m = mod[layer]
        return m[:, :d], m[:, d:2 * d], m[:, 2 * d:]

    shift, scale, gate = split(0)
    proj = _in_proj(x, p['norm_g'][0], scale, shift, w['hg_w_in'], zero_bias(5 * e))
    o = _hgrn2_dir(proj, w['hg_lb'], False, None)
    o = _hgrn2_dir(proj, w['hg_lb'], True, o)
    x = _out_proj(_mix_hgrn2, [o, proj, w['hg_norm_g'].reshape(1, e)],
                  lambda tm: [_row_spec(tm, e, 0), _row_spec(tm, e, 4), _vec_spec(e)],
                  w['hg_w_out'], x, gate)

    shift, scale, gate = split(1)
    proj = _in_proj(x, p['norm_g'][1], scale, shift, w['hy_w_in'], p['hy_b_in'][0], PROJ_DTYPE_NARROW)
    u, g1 = _hy_pre(proj, p['hy_conv_w'][0], p['hy_conv_b'][0])
    tables = _fft_tables(l)
    sd = _hy_filters(l, p['hy_f_w1'][0], p['hy_f_b1'][0], p['hy_f_w2'][0], p['hy_f_b2'][0],
                     w['hy_f_wout'], p['hy_f_freq'][0])
    spec = _hy_spectrum(sd, tables)
    yc = _hy_conv(u, spec, tables)
    x = _out_proj(_mix_hyena, [yc, u, g1, p['hy_skip'][0].reshape(1, e)],
                  lambda tm: [_row_spec(tm, e, 0)] * 3 + [_vec_spec(e)],
                  w['hy_w_out'], x, gate)

    shift, scale, gate = split(2)
    proj = _in_proj(x, p['norm_g'][2], scale, shift, w['rt_w_in'], zero_bias(2 * RT_QK + 2 * e),
                    PROJ_DTYPE_NARROW)
    cos, sin = _rope_tables(l)
    o = _ret_dir(proj, cos, sin, False, None)
    o = _ret_dir(proj, cos, sin, True, o)
    x = _out_proj(_mix_retention, [o, proj, p['rt_gn_g'][0].reshape(1, e)],
                  lambda tm: [_row_spec(tm, e, 0), _row_spec(tm, e, 2), _vec_spec(e)],
                  w['rt_w_out'], x, gate)

    shift, scale, gate = split(3)
    proj = _in_proj(x, p['norm_g'][3], scale, shift, w['lru_w_in'], zero_bias(2 * e))
    y = None
    for dirn in range(2):
        y = _lru_dir(proj, p['lru_conv_w'][0], p['lru_conv_b'][0], w['lru_gate_w'][dirn],
                     w['lru_gate_b'][dirn], p['lru_lambda'][0][dirn], dirn == 1, y)
    return _out_proj(_mix_lru, [y, proj], lambda tm: [_row_spec(tm, e, 0), _row_spec(tm, e, 1)],
                     w['lru_w_out'], x, gate, final_g=p['final_g'])


def kernel(x_prompt, x_sample, c_prompt, c_sample, ada_w, ada_b, norm_g, final_g, hg_lb, hg_w_in, hg_norm_g, hg_w_out, hy_w_in, hy_b_in, hy_conv_w, hy_conv_b, hy_f_w1, hy_f_b1, hy_f_w2, hy_f_b2, hy_f_wout, hy_f_freq, hy_skip, hy_w_out, rt_w_in, rt_gn_g, rt_w_out, lru_w_in, lru_conv_w, lru_conv_b, lru_gate_w, lru_gate_b, lru_lambda, lru_w_out):
    p = dict(ada_w=ada_w, ada_b=ada_b, norm_g=norm_g, final_g=final_g, hg_lb=hg_lb, hg_w_in=hg_w_in,
             hg_norm_g=hg_norm_g, hg_w_out=hg_w_out, hy_w_in=hy_w_in, hy_b_in=hy_b_in,
             hy_conv_w=hy_conv_w, hy_conv_b=hy_conv_b, hy_f_w1=hy_f_w1, hy_f_b1=hy_f_b1,
             hy_f_w2=hy_f_w2, hy_f_b2=hy_f_b2, hy_f_wout=hy_f_wout, hy_f_freq=hy_f_freq,
             hy_skip=hy_skip, hy_w_out=hy_w_out, rt_w_in=rt_w_in, rt_gn_g=rt_gn_g, rt_w_out=rt_w_out,
             lru_w_in=lru_w_in, lru_conv_w=lru_conv_w, lru_conv_b=lru_conv_b, lru_gate_w=lru_gate_w,
             lru_gate_b=lru_gate_b, lru_lambda=lru_lambda, lru_w_out=lru_w_out)
    w = _prep_weights(p)
    bp, bs = c_prompt.shape[0], c_sample.shape[0]
    rows = -(-(bp + bs) // SUBLANES) * SUBLANES
    c_all = jnp.concatenate([c_prompt, c_sample, jnp.zeros((rows - bp - bs, D_MODEL), F32)], axis=0)
    mod = _adaln(c_all.astype(F32), w['ada_w'], ada_b)
    y_prompt = _trunk(x_prompt, mod[:, :bp], p, w).astype(x_prompt.dtype)
    y_sample = _trunk(x_sample, mod[:, bp:bp + bs], p, w).astype(x_sample.dtype)
    return (y_prompt, y_sample)
```

```python
import functools
import math

import numpy as np
import jax
import jax.numpy as jnp
from jax import lax
from jax.experimental import pallas as pl
from jax.experimental.pallas import tpu as pltpu

F32 = jnp.float32
BF16 = jnp.bfloat16

D_MODEL = 1024
DEPTH = 4
E_WIDTH = 2 * D_MODEL
NORM_EPS = 1e-6
LANES = 128
SUBLANES = 8
MIB = 1024 * 1024
MIX_DTYPE = BF16
PROJ_DTYPE_NARROW = BF16

HG_CHUNK = 64
HG_DK = 128
HG_UNROLL = 4
OUT_PROJ_TM = 512
HG_HEADS = E_WIDTH // HG_DK

HY_EMB = 33
HY_BANDS = 16
HY_FH = 64
HY_INNER = 2
HY_FAST_DECAY = 0.3
HY_SLOW_DECAY = 1.5
HY_TARGET = 1e-2

RT_HEADS = 4
RT_QK = D_MODEL
RT_DK = RT_QK // RT_HEADS
RT_DV = E_WIDTH // RT_HEADS
RT_ROPE_BASE = 10000.0
RT_CHUNK = 256

LRU_CONV = 4
LRU_BLOCKS = 16
LRU_BS = E_WIDTH // LRU_BLOCKS
LRU_C = 8.0

_NT = (((1,), (1,)), ((), ()))


def _cparams(sem, vmem_mib):
    return pltpu.CompilerParams(dimension_semantics=sem, vmem_limit_bytes=vmem_mib * MIB)


def _bdot(a, b):
    return jnp.dot(a.astype(BF16), b.astype(BF16), preferred_element_type=F32)


def _bdot_nt(a, b):
    return lax.dot_general(a.astype(BF16), b.astype(BF16), _NT, preferred_element_type=F32)


def _sigmoid(x):
    return jax.nn.sigmoid(x)


def _silu(x):
    return x * _sigmoid(x)


def _expm1(x):
    u = jnp.exp(x)
    plain = jnp.logical_or(u == 1.0, x < -0.5)
    small = (u - 1.0) * x / jnp.where(plain, 1.0, jnp.log(u))
    return jnp.where(u == 1.0, x, jnp.where(x < -0.5, u - 1.0, small))


def _adaln_kernel(c_ref, w_ref, b_ref, o_ref):
    cs = _silu(c_ref[...])
    o_ref[...] = _bdot(cs, w_ref[...]) + b_ref[...]


def _adaln(c_all, ada_w16, ada_b):
    bp, d = c_all.shape
    tn = 1024
    return pl.pallas_call(
        _adaln_kernel,
        grid=(DEPTH, 3 * d // tn),
        in_specs=[pl.BlockSpec((bp, d), lambda l, j: (0, 0)),
                  pl.BlockSpec((None, d, tn), lambda l, j: (l, 0, j)),
                  pl.BlockSpec((None, 1, tn), lambda l, j: (l, 0, j))],
        out_specs=pl.BlockSpec((None, bp, tn), lambda l, j: (l, 0, j)),
        out_shape=jax.ShapeDtypeStruct((DEPTH, bp, 3 * d), F32),
        compiler_params=_cparams(("arbitrary", "arbitrary"), 32),
        name="adaln",
    )(c_all, ada_w16, ada_b.reshape(DEPTH, 1, 3 * d))


def _in_proj_kernel(x_ref, g_ref, sc_ref, sh_ref, w_ref, b_ref, o_ref, h_ref, *, tm):
    r0 = pl.multiple_of(pl.program_id(2) * tm, tm)

    @pl.when(pl.program_id(1) == 0)
    def _():
        x = x_ref[...]
        ms = jnp.mean(x * x, axis=-1, keepdims=True)
        h = x * lax.rsqrt(ms + NORM_EPS) * g_ref[...] * (1.0 + sc_ref[...]) + sh_ref[...]
        h_ref[pl.ds(r0, tm), :] = h.astype(BF16)

    o_ref[...] = (jnp.dot(h_ref[pl.ds(r0, tm), :], w_ref[...], preferred_element_type=F32)
                  + b_ref[...]).astype(o_ref.dtype)


IN_PROJ_TM = 1024
IN_PROJ_TN = 1024


def _in_proj(x, norm_g, scale, shift, w16, bias, out_dtype=F32):
    b, l, d = x.shape
    p = w16.shape[1]
    tm = min(l, IN_PROJ_TM)
    tn = IN_PROJ_TN
    ni = l // tm
    x_rows = lambda bi, j, i: (bi, jnp.where(j == 0, i, ni - 1), 0)
    return pl.pallas_call(
        functools.partial(_in_proj_kernel, tm=tm),
        grid=(b, p // tn, ni),
        in_specs=[pl.BlockSpec((None, tm, d), x_rows),
                  pl.BlockSpec((1, d), lambda bi, j, i: (0, 0)),
                  pl.BlockSpec((None, 1, d), lambda bi, j, i: (bi, 0, 0)),
                  pl.BlockSpec((None, 1, d), lambda bi, j, i: (bi, 0, 0)),
                  pl.BlockSpec((d, tn), lambda bi, j, i: (0, j)),
                  pl.BlockSpec((1, tn), lambda bi, j, i: (0, j))],
        out_specs=pl.BlockSpec((None, tm, tn), lambda bi, j, i: (bi, i, j)),
        out_shape=jax.ShapeDtypeStruct((b, l, p), out_dtype),
        scratch_shapes=[pltpu.VMEM((l, d), BF16)],
        compiler_params=_cparams(("arbitrary", "arbitrary", "arbitrary"), 48),
        name="in_proj",
    )(x, norm_g.reshape(1, d), scale.reshape(b, 1, d), shift.reshape(b, 1, d), w16,
      bias.reshape(1, p))


def _ld(ref):
    return ref[...].astype(F32)


def _head_rms(o, width):
    parts = []
    for s in range(0, o.shape[1], width):
        oh = o[:, s:s + width]
        ms = jnp.mean(oh * oh, axis=-1, keepdims=True)
        parts.append(oh * lax.rsqrt(ms + NORM_EPS))
    return jnp.concatenate(parts, axis=1)


def _mix_hgrn2(o_ref, z_ref, g_ref):
    return (_head_rms(_ld(o_ref), HG_DK) * g_ref[...]) * _silu(_ld(z_ref))


def _mix_hyena(yc_ref, u_ref, g1_ref, skip_ref):
    return _ld(g1_ref) * (_ld(yc_ref) + _ld(u_ref) * skip_ref[...])


def _mix_retention(o_ref, z_ref, g_ref):
    return (_head_rms(_ld(o_ref), RT_DV) * g_ref[...]) * _silu(_ld(z_ref))


def _mix_lru(y_ref, z_ref):
    return _ld(y_ref) * _silu(_ld(z_ref))


def _out_proj_kernel(*refs, mix, n_mix, final):
    mix_refs = refs[:n_mix]
    w_ref, x_ref, gate_ref = refs[n_mix:n_mix + 3]
    o_ref = refs[-1]
    y = mix(*mix_refs)
    out = x_ref[...] + gate_ref[...] * _bdot(y, w_ref[...])
    if final:
        fg_ref = refs[n_mix + 3]
        ms = jnp.mean(out * out, axis=-1, keepdims=True)
        out = out * lax.rsqrt(ms + NORM_EPS) * fg_ref[...]
    o_ref[...] = out


def _out_proj(mix, mix_args, mix_specs, w16, x, gate, final_g=None):
    b, l, d = x.shape
    e = w16.shape[0]
    tm = min(l, OUT_PROJ_TM)
    in_specs = list(mix_specs(tm)) + [
        pl.BlockSpec((e, d), lambda bi, i: (0, 0)),
        pl.BlockSpec((None, tm, d), lambda bi, i: (bi, i, 0)),
        pl.BlockSpec((None, 1, d), lambda bi, i: (bi, 0, 0))]
    args = list(mix_args) + [w16, x, gate.reshape(b, 1, d)]
    if final_g is not None:
        in_specs.append(pl.BlockSpec((1, d), lambda bi, i: (0, 0)))
        args.append(final_g.reshape(1, d))
    return pl.pallas_call(
        functools.partial(_out_proj_kernel, mix=mix, n_mix=len(mix_args), final=final_g is not None),
        grid=(b, l // tm),
        in_specs=in_specs,
        out_specs=pl.BlockSpec((None, tm, d), lambda bi, i: (bi, i, 0)),
        out_shape=jax.ShapeDtypeStruct((b, l, d), F32),
        compiler_params=_cparams(("arbitrary", "arbitrary"), 48),
        name="out_proj",
    )(*args)


def _row_spec(tm, width, col):
    return pl.BlockSpec((None, tm, width), lambda bi, i: (bi, i, col))


def _vec_spec(width):
    return pl.BlockSpec((1, width), lambda bi, i: (0, 0))


def _cumsum_rows(x, reverse):
    sub = lax.broadcasted_iota(jnp.int32, (SUBLANES, x.shape[1]), 0)
    groups = x.shape[0] // SUBLANES
    order = range(groups - 1, -1, -1) if reverse else range(groups)
    edge = 0 if reverse else SUBLANES - 1
    out = [None] * groups
    total = None
    for gi in order:
        xg = x[gi * SUBLANES:(gi + 1) * SUBLANES, :]
        s = 1
        while s < SUBLANES:
            valid = (sub < SUBLANES - s) if reverse else (sub >= s)
            xg = xg + jnp.where(valid, pltpu.roll(xg, (SUBLANES - s) if reverse else s, 0), 0.0)
            s *= 2
        if total is not None:
            xg = xg + total
        total = xg[edge:edge + 1, :]
        out[gi] = xg
    return jnp.concatenate(out, axis=0)


def _hgrn2_kernel(*refs, reverse, hb, nch, add):
    q_ref, f_ref, v_ref, lb_ref = refs[:4]
    prev_ref = refs[4] if add else None
    o_ref, st_ref = refs[-2:]

    @pl.when(pl.program_id(2) == 0)
    def _():
        st_ref[...] = jnp.zeros_like(st_ref)

    lb_exp = jnp.exp(lb_ref[...] - jnp.max(lb_ref[...], axis=0, keepdims=True))
    lb_all = lb_exp[0:1, :] / jnp.sum(lb_exp, axis=0, keepdims=True)

    c = HG_CHUNK
    row = lax.broadcasted_iota(jnp.int32, (c, c), 0)
    col = lax.broadcasted_iota(jnp.int32, (c, c), 1)
    mask = (col >= row) if reverse else (col <= row)
    mid = c // 2
    ref_row = (c - 1 - mid) if reverse else mid
    last_row = 0 if reverse else c - 1

    def chunk(ci, carry):
        cc = (nch - 1 - ci) if reverse else ci
        r0 = pl.multiple_of(cc * c, c)
        for hh in range(hb):
            sl = slice(hh * HG_DK, (hh + 1) * HG_DK)
            q = _silu(q_ref[pl.ds(r0, c), sl])
            lb = lb_all[:, sl]
            f = lb + (1.0 - lb) * _sigmoid(f_ref[pl.ds(r0, c), sl])
            k = 1.0 - f
            g = jnp.log(f)
            v = v_ref[pl.ds(r0, c), sl]
            bsum = _cumsum_rows(g, reverse)
            b_ref_row = bsum[ref_row:ref_row + 1, :]
            b_last = bsum[last_row:last_row + 1, :]
            scores = _bdot_nt(q * jnp.exp(bsum - b_ref_row), k * jnp.exp(b_ref_row - bsum))
            scores = jnp.where(mask, scores, 0.0)
            st = st_ref[hh]
            vt = v.T
            o = _bdot_nt(jnp.concatenate([q * jnp.exp(bsum), scores], axis=1),
                         jnp.concatenate([st, vt], axis=1))
            st_ref[hh] = st * jnp.exp(b_last) + _bdot(vt, k * jnp.exp(b_last - bsum))
            if add:
                o = o + prev_ref[pl.ds(r0, c), sl].astype(F32)
            o_ref[pl.ds(r0, c), sl] = o.astype(o_ref.dtype)
        return carry

    lax.fori_loop(0, nch, chunk, 0, unroll=HG_UNROLL)


def _hgrn2_dir(proj, lb, reverse, prev):
    b, l, _ = proj.shape
    e = E_WIDTH
    hb = 8
    w = hb * HG_DK
    t = min(l, 512)
    nt = l // t
    ncol = e // w
    fsec = 2 if reverse else 1

    def rows(bi, h, ti):
        return (nt - 1 - ti) if reverse else ti

    def sec(s):
        return pl.BlockSpec((None, t, w), lambda bi, h, ti: (bi, rows(bi, h, ti), s * ncol + h))

    in_specs = [sec(0), sec(fsec), sec(3), pl.BlockSpec((DEPTH + 1, w), lambda bi, h, ti: (0, h))]
    args = [proj, proj, proj, lb]
    out_spec = pl.BlockSpec((None, t, w), lambda bi, h, ti: (bi, rows(bi, h, ti), h))
    if prev is not None:
        in_specs.append(out_spec)
        args.append(prev)
    return pl.pallas_call(
        functools.partial(_hgrn2_kernel, reverse=reverse, hb=hb, nch=t // HG_CHUNK,
                          add=prev is not None),
        grid=(b, ncol, nt),
        in_specs=in_specs,
        out_specs=out_spec,
        out_shape=jax.ShapeDtypeStruct((b, l, e), MIX_DTYPE),
        scratch_shapes=[pltpu.VMEM((hb, HG_DK, HG_DK), F32)],
        compiler_params=_cparams(("arbitrary", "arbitrary", "arbitrary"), 32),
        name="hgrn2_bwd" if reverse else "hgrn2_fwd",
    )(*args)


def _ret_tables(reverse):
    c = RT_CHUNK
    hidx = np.arange(RT_HEADS, dtype=np.float64)
    lg = np.log1p(-np.exp2((-5.5 if reverse else -5.0) - hidx))[:, None]
    pos = np.arange(c, dtype=np.float64)[None, :]
    rel = pos[0][:, None] - pos[0][None, :]
    if reverse:
        rel = -rel
    decay = np.where(rel >= 0, np.exp(lg[:, :, None] * np.maximum(rel, 0.0)[None]), 0.0)
    q_dec = np.exp(lg * ((c - pos) if reverse else (pos + 1.0)))
    k_dec = np.exp(lg * (pos if reverse else (c - 1.0 - pos)))
    c_dec = np.exp(lg * c)
    return (jnp.asarray(decay, F32),
            jnp.asarray(np.broadcast_to(q_dec[:, :, None], (RT_HEADS, c, RT_DV)), F32),
            jnp.asarray(np.broadcast_to(k_dec[:, :, None], (RT_HEADS, c, RT_DK)), F32),
            jnp.asarray(np.broadcast_to(c_dec[:, :, None], (RT_HEADS, 1, RT_DV)), F32))


def _rope_tables(l):
    inv = RT_ROPE_BASE ** (-jnp.arange(0, RT_DK, 2, dtype=F32) / RT_DK)
    ang = jnp.arange(l, dtype=F32)[:, None] * inv[None]
    return jnp.cos(ang), jnp.sin(ang)


def _ret_kernel(*refs, add):
    q_ref, k_ref, v_ref, cos_ref, sin_ref, dec_ref, qd_ref, kd_ref, cd_ref = refs[:9]
    prev_ref = refs[9] if add else None
    o_ref, r_ref = refs[-2:]

    @pl.when(pl.program_id(1) == 0)
    def _():
        r_ref[...] = jnp.zeros_like(r_ref)

    cos = cos_ref[...]
    sin = sin_ref[...]
    half = RT_DK // 2

    def rot(t):
        t1 = t[:, :half]
        t2 = t[:, half:]
        return jnp.concatenate([t1 * cos - t2 * sin, t1 * sin + t2 * cos], axis=1)

    for h in range(RT_HEADS):
        qs = slice(h * RT_DK, (h + 1) * RT_DK)
        vs = slice(h * RT_DV, (h + 1) * RT_DV)
        q = rot(q_ref[:, qs].astype(F32))
        k = rot(k_ref[:, qs].astype(F32)) * (RT_DK ** -0.5)
        v = v_ref[:, vs]
        scores = _bdot_nt(q, k) * dec_ref[h]
        r = r_ref[h]
        o = _bdot(scores, v) + qd_ref[h] * _bdot(q, r)
        r_ref[h] = cd_ref[h] * r + _bdot((k * kd_ref[h]).T, v)
        if add:
            o = o + prev_ref[:, vs].astype(F32)
        o_ref[:, vs] = o.astype(o_ref.dtype)


def _ret_dir(proj, cos, sin, reverse, prev):
    b, l, _ = proj.shape
    c = RT_CHUNK
    nt = l // c
    e = E_WIDTH
    dec, qd, kd, cd = _ret_tables(reverse)

    def rows(ti):
        return (nt - 1 - ti) if reverse else ti

    whole = lambda a: pl.BlockSpec(a.shape, lambda bi, ti: (0, 0, 0))
    in_specs = [
        pl.BlockSpec((None, c, RT_QK), lambda bi, ti: (bi, rows(ti), 0)),
        pl.BlockSpec((None, c, RT_QK), lambda bi, ti: (bi, rows(ti), 1)),
        pl.BlockSpec((None, c, e), lambda bi, ti: (bi, rows(ti), 2 * RT_QK // e)),
        pl.BlockSpec((c, RT_DK // 2), lambda bi, ti: (rows(ti), 0)),
        pl.BlockSpec((c, RT_DK // 2), lambda bi, ti: (rows(ti), 0)),
        whole(dec), whole(qd), whole(kd), whole(cd),
    ]
    args = [proj, proj, proj, cos, sin, dec, qd, kd, cd]
    out_spec = pl.BlockSpec((None, c, e), lambda bi, ti: (bi, rows(ti), 0))
    if prev is not None:
        in_specs.append(out_spec)
        args.append(prev)
    return pl.pallas_call(
        functools.partial(_ret_kernel, add=prev is not None),
        grid=(b, nt),
        in_specs=in_specs,
        out_specs=out_spec,
        out_shape=jax.ShapeDtypeStruct((b, l, e), MIX_DTYPE),
        scratch_shapes=[pltpu.VMEM((RT_HEADS, RT_DK, RT_DV), F32)],
        compiler_params=_cparams(("arbitrary", "arbitrary"), 48),
        name="ret_bwd" if reverse else "ret_fwd",
    )(*args)


def _halo_rows(dtype):
    return SUBLANES * (4 // jnp.dtype(dtype).itemsize)


def _halo_specs(t, w, l, col, order, hr=SUBLANES):
    per = t // hr
    nblk = l // hr
    prev = pl.BlockSpec((None, hr, w),
                        lambda *g: (g[0], jnp.maximum(order(*g) * per - 1, 0), col(*g)))
    nxt = pl.BlockSpec((None, hr, w),
                       lambda *g: (g[0], jnp.minimum((order(*g) + 1) * per, nblk - 1), col(*g)))
    return prev, nxt


def _fill_ext(ext_ref, x_ref, xp_ref, xn_ref, first, last, t):
    hr = xp_ref.shape[0]
    ext_ref[0:SUBLANES, :] = jnp.where(first, 0.0, xp_ref[hr - SUBLANES:hr, :].astype(F32))
    ext_ref[SUBLANES:SUBLANES + t, :] = x_ref[...].astype(F32)
    ext_ref[SUBLANES + t:2 * SUBLANES + t, :] = jnp.where(last, 0.0, xn_ref[0:SUBLANES, :].astype(F32))


def _lru_pitches(t):
    seg = t // SUBLANES
    return seg, seg + 3 * SUBLANES, seg + SUBLANES


def _lru_kernel(*refs, reverse, t, nt, add):
    x_ref, xp_ref, xn_ref, cw_ref, cb_ref, gw_ref, gb_ref, lam_ref = refs[:8]
    prev_ref = refs[8] if add else None
    o_ref, ext_ref, hbuf_ref, carry_ref = refs[-4:]
    ti = pl.program_id(1)
    te = (nt - 1 - ti) if reverse else ti
    seg, pin, pout = _lru_pitches(t)
    nseg = SUBLANES
    left = LRU_CONV // 2

    @pl.when(ti == 0)
    def _():
        carry_ref[...] = jnp.zeros_like(carry_ref)

    neg_lam = -lam_ref[...]
    softplus = jnp.maximum(neg_lam, 0.0) + jnp.log1p(jnp.exp(-jnp.abs(neg_lam)))
    jorder = range(seg - 1, -1, -1) if reverse else range(seg)
    sorder = range(nseg - 1, -1, -1) if reverse else range(nseg)
    edge = 0 if reverse else seg - 1

    for n in range(LRU_BLOCKS):
        sl = slice(n * LRU_BS, (n + 1) * LRU_BS)
        for s in range(nseg):
            r0 = s * seg
            before = (jnp.where(te == 0, 0.0, xp_ref[:, sl]) if s == 0
                      else x_ref[r0 - SUBLANES:r0, sl])
            after = (jnp.where(te == nt - 1, 0.0, xn_ref[:, sl]) if s == nseg - 1
                     else x_ref[r0 + seg:r0 + seg + SUBLANES, sl])
            ext_ref[n, s * pin:s * pin + SUBLANES, :] = before
            ext_ref[n, s * pin + SUBLANES:s * pin + SUBLANES + seg, :] = x_ref[r0:r0 + seg, sl]
            ext_ref[n, s * pin + SUBLANES + seg:s * pin + 2 * SUBLANES + seg, :] = after
        xs = []
        for j in range(seg):
            acc = cb_ref[:, sl]
            for d in range(LRU_CONV):
                acc = acc + cw_ref[d:d + 1, sl] * ext_ref[n, pl.ds(SUBLANES + j + d - left, nseg, stride=pin), :]
            xs.append(acc)
        xn = jnp.concatenate(xs, axis=0)
        gates = _bdot(xn, gw_ref[n]) + gb_ref[n]
        r = _sigmoid(gates[:, :LRU_BS])
        i = _sigmoid(gates[:, LRU_BS:])
        log_a = -LRU_C * r * softplus[:, sl]
        a = jnp.exp(log_a)
        bb = jnp.sqrt(-_expm1(2.0 * log_a)) * (i * xn)
        hs = [None] * seg
        ps = [None] * seg
        h = p = None
        for j in jorder:
            aj = a[j * nseg:(j + 1) * nseg, :]
            bj = bb[j * nseg:(j + 1) * nseg, :]
            h = bj if h is None else aj * h + bj
            p = aj if p is None else aj * p
            hs[j], ps[j] = h, p
        c = carry_ref[:, sl]
        enter = [None] * nseg
        for s in sorder:
            enter[s] = c
            c = hs[edge][s:s + 1, :] + ps[edge][s:s + 1, :] * c
        carry_ref[:, sl] = c
        cin = jnp.concatenate(enter, axis=0)
        for j in range(seg):
            hbuf_ref[n, pl.ds(j, nseg, stride=pout), :] = hs[j] + ps[j] * cin
        for s in range(nseg):
            hn = hbuf_ref[n, s * pout:s * pout + seg, :]
            if add:
                hn = hn + prev_ref[s * seg:(s + 1) * seg, sl].astype(F32)
            o_ref[s * seg:(s + 1) * seg, sl] = hn.astype(o_ref.dtype)


def _lru_dir(proj, conv_w, conv_b, gate_w16, gate_b, lam, reverse, prev):
    b, l, _ = proj.shape
    e = E_WIDTH
    t = min(l, 256)
    nt = l // t

    def order(bi, ti):
        return (nt - 1 - ti) if reverse else ti

    xp_spec, xn_spec = _halo_specs(t, e, l, lambda bi, ti: 0, order)
    in_specs = [
        pl.BlockSpec((None, t, e), lambda bi, ti: (bi, order(bi, ti), 0)), xp_spec, xn_spec,
        pl.BlockSpec((LRU_CONV, e), lambda bi, ti: (0, 0)),
        pl.BlockSpec((1, e), lambda bi, ti: (0, 0)),
        pl.BlockSpec((LRU_BLOCKS, LRU_BS, 2 * LRU_BS), lambda bi, ti: (0, 0, 0)),
        pl.BlockSpec((LRU_BLOCKS, 1, 2 * LRU_BS), lambda bi, ti: (0, 0, 0)),
        pl.BlockSpec((1, e), lambda bi, ti: (0, 0)),
    ]
    args = [proj, proj, proj, conv_w, conv_b.reshape(1, e), gate_w16, gate_b, lam.reshape(1, e)]
    out_spec = pl.BlockSpec((None, t, e), lambda bi, ti: (bi, order(bi, ti), 0))
    if prev is not None:
        in_specs.append(out_spec)
        args.append(prev)
    return pl.pallas_call(
        functools.partial(_lru_kernel, reverse=reverse, t=t, nt=nt, add=prev is not None),
        grid=(b, nt),
        in_specs=in_specs,
        out_specs=out_spec,
        out_shape=jax.ShapeDtypeStruct((b, l, e), MIX_DTYPE),
        scratch_shapes=[pltpu.VMEM((LRU_BLOCKS, SUBLANES * _lru_pitches(t)[1], LRU_BS), F32),
                        pltpu.VMEM((LRU_BLOCKS, SUBLANES * _lru_pitches(t)[2], LRU_BS), F32),
                        pltpu.VMEM((1, e), F32)],
        compiler_params=_cparams(("arbitrary", "arbitrary"), 48),
        name="lru_bwd" if reverse else "lru_fwd",
    )(*args)


def _hy_pre_kernel(x0_ref, x0p_ref, x0n_ref, x1_ref, x1p_ref, x1n_ref, v_ref, vp_ref, vn_ref,
                   z_ref, w0_ref, w1_ref, wv_ref, b0_ref, b1_ref, bv_ref,
                   u_ref, g1_ref, e0_ref, e1_ref, ev_ref, *, t, nt):
    ti = pl.program_id(1)
    first = ti == 0
    last = ti == nt - 1

    def conv(x_ref, xp_ref, xn_ref, ext_ref, w_ref, b_ref):
        _fill_ext(ext_ref, x_ref, xp_ref, xn_ref, first, last, t)
        out = b_ref[...]
        for j in range(3):
            out = out + w_ref[j:j + 1, :] * ext_ref[pl.ds(SUBLANES - 1 + j, t), :]
        return out

    x0 = conv(x0_ref, x0p_ref, x0n_ref, e0_ref, w0_ref, b0_ref)
    x1 = conv(x1_ref, x1p_ref, x1n_ref, e1_ref, w1_ref, b1_ref)
    v = conv(v_ref, vp_ref, vn_ref, ev_ref, wv_ref, bv_ref)
    u_ref[...] = x0 * v
    g1_ref[...] = (x1 * _silu(z_ref[...].astype(F32))).astype(g1_ref.dtype)


def _hy_pre(proj, conv_w, conv_b):
    b, l, _ = proj.shape
    e = E_WIDTH
    w = 512
    t = min(l, 512)
    nt = l // t
    ncol = e // w

    def order(bi, ti, j):
        return ti

    specs, args = [], []
    for s in range(3):
        col = (lambda s: lambda bi, ti, j: s * ncol + j)(s)
        xp, xn = _halo_specs(t, w, l, col, order, _halo_rows(proj.dtype))
        specs += [pl.BlockSpec((None, t, w), (lambda col: lambda bi, ti, j: (bi, ti, col(bi, ti, j)))(col)),
                  xp, xn]
        args += [proj, proj, proj]
    specs.append(pl.BlockSpec((None, t, w), lambda bi, ti, j: (bi, ti, 3 * ncol + j)))
    args.append(proj)
    for s in range(3):
        specs.append(pl.BlockSpec((3, w), (lambda s: lambda bi, ti, j: (0, s * ncol + j))(s)))
        args.append(conv_w)
    cb = conv_b.reshape(1, 3 * e)
    for s in range(3):
        specs.append(pl.BlockSpec((1, w), (lambda s: lambda bi, ti, j: (0, s * ncol + j))(s)))
        args.append(cb)
    out_spec = pl.BlockSpec((None, t, w), lambda bi, ti, j: (bi, ti, j))
    return pl.pallas_call(
        functools.partial(_hy_pre_kernel, t=t, nt=nt),
        grid=(b, nt, ncol),
        in_specs=specs,
        out_specs=[out_spec, out_spec],
        out_shape=[jax.ShapeDtypeStruct((b, l, e), F32), jax.ShapeDtypeStruct((b, l, e), MIX_DTYPE)],
        scratch_shapes=[pltpu.VMEM((t + 2 * SUBLANES, w), F32)] * 3,
        compiler_params=_cparams(("arbitrary", "arbitrary", "arbitrary"), 48),
        name="hy_pre",
    )(*args)


def _hy_filter_kernel(z_ref, w1_ref, b1_ref, w2_ref, b2_ref, fr_ref, wf_ref, wb_ref, dl_ref,
                      sd_ref, a_ref):
    @pl.when(pl.program_id(1) == 0)
    def _():
        fr = fr_ref[...]
        a = jnp.sin(fr * (_bdot(z_ref[...], w1_ref[...]) + b1_ref[...]))
        for j in range(HY_INNER):
            a = jnp.sin(fr * (_bdot(a, w2_ref[j]) + b2_ref[j]))
        a_ref[...] = a

    a = a_ref[...]
    window = jnp.exp(-z_ref[:, 0:1] * dl_ref[...])
    h_fw = _bdot(a, wf_ref[...]) * window
    h_bw = _bdot(a, wb_ref[...]) * window
    sd_ref[0] = h_fw + h_bw
    sd_ref[1] = h_fw - h_bw


def _hy_filters(l, w1, b1, w2, b2, wout16, freq):
    e = E_WIDTH
    kp = LANES
    t = jnp.linspace(0.0, 1.0, l, dtype=F32)[:, None]
    wv = 2.0 * math.pi * jnp.arange(l, dtype=F32)[:, None] / l
    bands = jnp.linspace(1e-4, HY_BANDS - 1, HY_BANDS, dtype=F32)[None]
    z = jnp.concatenate([t, jnp.cos(bands * wv), -jnp.sin(bands * wv),
                         jnp.zeros((l, kp - HY_EMB), F32)], axis=-1)
    w1p = jnp.concatenate([w1, jnp.zeros((kp - HY_EMB, HY_FH), F32)], axis=0)
    max_decay = math.log(HY_TARGET) / HY_FAST_DECAY
    min_decay = math.log(HY_TARGET) / HY_SLOW_DECAY
    deltas = jnp.abs(jnp.linspace(min_decay, max_decay, e, dtype=F32))[None]
    tm = min(l, 512)
    w = 512
    ncol = e // w
    return pl.pallas_call(
        _hy_filter_kernel,
        grid=(l // tm, ncol),
        in_specs=[pl.BlockSpec((tm, kp), lambda i, j: (i, 0)),
                  pl.BlockSpec((kp, HY_FH), lambda i, j: (0, 0)),
                  pl.BlockSpec((1, HY_FH), lambda i, j: (0, 0)),
                  pl.BlockSpec((HY_INNER, HY_FH, HY_FH), lambda i, j: (0, 0, 0)),
                  pl.BlockSpec((HY_INNER, 1, HY_FH), lambda i, j: (0, 0, 0)),
                  pl.BlockSpec((1, HY_FH), lambda i, j: (0, 0)),
                  pl.BlockSpec((HY_FH, w), lambda i, j: (0, j)),
                  pl.BlockSpec((HY_FH, w), lambda i, j: (0, ncol + j)),
                  pl.BlockSpec((1, w), lambda i, j: (0, j))],
        out_specs=pl.BlockSpec((2, tm, w), lambda i, j: (0, i, j)),
        out_shape=jax.ShapeDtypeStruct((2, l, e), F32),
        scratch_shapes=[pltpu.VMEM((tm, HY_FH), F32)],
        compiler_params=_cparams(("arbitrary", "arbitrary"), 32),
        name="hy_filter",
    )(z, w1p, b1.reshape(1, HY_FH), w2, b2.reshape(HY_INNER, 1, HY_FH), freq.reshape(1, HY_FH),
      wout16, wout16, deltas)


FFT_UNROLL = 16


def _fft_dims(l):
    n = 2 * l
    n1 = int(round(math.sqrt(n)))
    assert n1 * n1 == n and n1 % 16 == 0, "sequence length must give a square DFT factorisation"
    return n1, n1


def _fft_pitch(n1):
    return 2 * n1 + SUBLANES


def _fft_tables(l):
    n1, n2 = _fft_dims(l)
    n = n1 * n2
    k1 = np.arange(n1)[:, None]
    m1 = np.arange(n1 // 2)[None, :]
    j2 = np.arange(n2)[:, None, None]
    ang = -2.0 * np.pi * (k1 * m1 / n1)[None] - 2.0 * np.pi * (j2 * k1[None] / n)
    gr, gi = np.cos(ang), np.sin(ang)
    g_fwd = np.concatenate([gr, gi], axis=1)
    g_fwd2 = np.concatenate([np.concatenate([gr, -gi], axis=2), np.concatenate([gi, gr], axis=2)], axis=1)
    ang_i = 2.0 * np.pi * (m1.T * k1.T / n1)[None] + 2.0 * np.pi * (j2 * k1.T[None] / n)
    er, ei = np.cos(ang_i) / n, np.sin(ang_i) / n
    g_inv2 = np.concatenate([np.concatenate([er, -ei], axis=2), np.concatenate([ei, er], axis=2)], axis=1)
    a2 = -2.0 * np.pi * np.arange(n2)[:, None] * np.arange(n2)[None, :] / n2
    fr, fi = np.cos(a2), np.sin(a2)
    f2 = np.block([[fr, -fi], [fi, fr]])
    f2_inv = np.block([[fr, fi], [-fi, fr]])
    f2_half = np.stack([np.concatenate([fr, -fi], axis=1), np.concatenate([fi, fr], axis=1)])
    as16 = lambda a: jnp.asarray(a, F32).astype(BF16)
    return dict(g_fwd=as16(g_fwd), g_fwd2=as16(g_fwd2), g_inv2=as16(g_inv2), f2=as16(f2),
                f2_inv=as16(f2_inv), f2_half=as16(f2_half))


def _fft_stage1(x_ref, g_ref, work_ref, n2_lo, cnt, n1, n2, pitch):
    def body(j, carry):
        jj = n2_lo + j
        xs = x_ref[pl.ds(jj, n1 // 2, stride=n2), :]
        r0 = pl.multiple_of(jj * pitch, SUBLANES)
        work_ref[pl.ds(r0, 2 * n1), :] = _bdot(g_ref[j], xs)
        return carry
    lax.fori_loop(0, cnt, body, 0, unroll=FFT_UNROLL)


def _fft_load_k1(work_ref, k1, n1, n2, pitch):
    br = work_ref[pl.ds(k1, n2, stride=pitch), :]
    bi = work_ref[pl.ds(n1 + k1, n2, stride=pitch), :]
    return jnp.concatenate([br, bi], axis=0)


def _hy_spec_kernel(x_ref, g_ref, f2_ref, t_ref, work_ref, *, n1, n2, nc):
    p = pl.program_id(2)
    pitch = _fft_pitch(n1)
    c2 = n2 // nc
    c1 = n1 // nc

    @pl.when(p < nc)
    def _():
        _fft_stage1(x_ref, g_ref, work_ref, p * c2, c2, n1, n2, pitch)

    @pl.when(p >= nc)
    def _():
        def body(j, carry):
            k1 = (p - nc) * c1 + j
            t_ref[j] = _bdot(f2_ref[...], _fft_load_k1(work_ref, k1, n1, n2, pitch))
            return carry
        lax.fori_loop(0, c1, body, 0, unroll=FFT_UNROLL)


def _fft_nc(l):
    return 8 if l >= 8192 else (2 if l >= 2048 else 1)


def _hy_spectrum(sd, tables):
    _, l, e = sd.shape
    n1, n2 = _fft_dims(l)
    nc = _fft_nc(l)
    g_fwd, f2_half = tables['g_fwd'], tables['f2_half']
    pitch = _fft_pitch(n1)
    return pl.pallas_call(
        functools.partial(_hy_spec_kernel, n1=n1, n2=n2, nc=nc),
        grid=(e // LANES, 2, 2 * nc),
        in_specs=[pl.BlockSpec((None, l, LANES), lambda c, j, p: (j, 0, c)),
                  pl.BlockSpec((n2 // nc, 2 * n1, n1 // 2), lambda c, j, p: (jnp.minimum(p, nc - 1), 0, 0)),
                  pl.BlockSpec((None, n2, 2 * n2), lambda c, j, p: (j, 0, 0))],
        out_specs=pl.BlockSpec((n1 // nc, n2, LANES),
                               lambda c, j, p: (jnp.maximum(p - nc, 0), j, c)),
        out_shape=jax.ShapeDtypeStruct((n1, 2 * n2, e), F32),
        scratch_shapes=[pltpu.VMEM((n2 * pitch, LANES), F32)],
        compiler_params=_cparams(("arbitrary", "arbitrary", "arbitrary"), 48),
        name="hy_spectrum",
    )(sd, g_fwd, f2_half)


def _hy_conv_kernel(u_ref, gf_ref, f2_ref, f2i_ref, t_ref, gi_ref, y_ref, work_ref, *, n1, n2, nc):
    p = pl.program_id(2)
    pitch = _fft_pitch(n1)
    c2 = n2 // nc
    c1 = n1 // nc

    @pl.when(p < nc)
    def _():
        def body(j, carry):
            jj = p * c2 + j
            xs = jnp.concatenate([u_ref[0, pl.ds(jj, n1 // 2, stride=n2), :],
                                  u_ref[1, pl.ds(jj, n1 // 2, stride=n2), :]], axis=0)
            r0 = pl.multiple_of(jj * pitch, SUBLANES)
            work_ref[pl.ds(r0, 2 * n1), :] = _bdot(gf_ref[j], xs)
            return carry
        lax.fori_loop(0, c2, body, 0, unroll=FFT_UNROLL)

    @pl.when(jnp.logical_and(p >= nc, p < 2 * nc))
    def _():
        def body(j, carry):
            k1 = (p - nc) * c1 + j
            x = _bdot(f2_ref[...], _fft_load_k1(work_ref, k1, n1, n2, pitch))
            xr, xi = x[:n2], x[n2:]
            tr, ti = t_ref[j, :n2, :], t_ref[j, n2:, :]
            z = jnp.concatenate([xr * tr - xi * ti, xr * ti + xi * tr], axis=0)
            cmat = _bdot(f2i_ref[...], z)
            work_ref[pl.ds(k1, n2, stride=pitch), :] = cmat[:n2]
            work_ref[pl.ds(n1 + k1, n2, stride=pitch), :] = cmat[n2:]
            return carry
        lax.fori_loop(0, c1, body, 0, unroll=FFT_UNROLL)

    @pl.when(p >= 2 * nc)
    def _():
        def body(j, carry):
            jj = (p - 2 * nc) * c2 + j
            r0 = pl.multiple_of(jj * pitch, SUBLANES)
            d = work_ref[pl.ds(r0, 2 * n1), :]
            y = _bdot(gi_ref[j], d)
            y_ref[0, pl.ds(jj, n1 // 2, stride=n2), :] = y[:n1 // 2]
            y_ref[1, pl.ds(jj, n1 // 2, stride=n2), :] = y[n1 // 2:]
            return carry
        lax.fori_loop(0, c2, body, 0, unroll=FFT_UNROLL)


def _hy_conv(u, spec, tables):
    b, l, e = u.shape
    assert b % 2 == 0, "batch rows are transformed in pairs"
    n1, n2 = _fft_dims(l)
    nc = _fft_nc(l)
    pitch = _fft_pitch(n1)
    clip = lambda v: jnp.clip(v, 0, nc - 1)
    pair_spec = lambda **kw: pl.BlockSpec((2, l, LANES), lambda c, bi, p: (bi, 0, c), **kw)
    return pl.pallas_call(
        functools.partial(_hy_conv_kernel, n1=n1, n2=n2, nc=nc),
        grid=(e // LANES, b // 2, 3 * nc),
        in_specs=[pair_spec(),
                  pl.BlockSpec((n2 // nc, 2 * n1, n1), lambda c, bi, p: (clip(p), 0, 0)),
                  pl.BlockSpec((2 * n2, 2 * n2), lambda c, bi, p: (0, 0)),
                  pl.BlockSpec((2 * n2, 2 * n2), lambda c, bi, p: (0, 0)),
                  pl.BlockSpec((n1 // nc, 2 * n2, LANES), lambda c, bi, p: (clip(p - nc), 0, c)),
                  pl.BlockSpec((n2 // nc, n1, 2 * n1), lambda c, bi, p: (clip(p - 2 * nc), 0, 0))],
        out_specs=pair_spec(pipeline_mode=pl.Buffered(1)),
        out_shape=jax.ShapeDtypeStruct((b, l, e), F32),
        scratch_shapes=[pltpu.VMEM((n2 * pitch, LANES), F32)],
        compiler_params=_cparams(("arbitrary", "arbitrary", "arbitrary"), 56),
        name="hy_conv",
    )(u, tables['g_fwd2'], tables['f2'], tables['f2_inv'], spec, tables['g_inv2'])


def _prep_weights(p):
    c16 = lambda a: a.astype(BF16)
    gw = p['lru_gate_w'][0]
    gw = jnp.concatenate([gw[:, 0], gw[:, 1]], axis=-1)
    gb = p['lru_gate_b'][0].reshape(2, 2, LRU_BLOCKS, 1, LRU_BS)
    gb = jnp.concatenate([gb[:, 0], gb[:, 1]], axis=-1)
    lb = p['hg_lb'].astype(F32)
    return dict(
        ada_w=c16(p['ada_w']), hg_w_in=c16(p['hg_w_in'][0]), hg_w_out=c16(p['hg_w_out'][0]),
        hy_w_in=c16(p['hy_w_in'][0]), hy_w_out=c16(p['hy_w_out'][0]), hy_f_wout=c16(p['hy_f_wout'][0]),
        rt_w_in=c16(p['rt_w_in'][0]), rt_w_out=c16(p['rt_w_out'][0]),
        lru_w_in=c16(p['lru_w_in'][0]), lru_w_out=c16(p['lru_w_out'][0]),
        lru_gate_w=c16(gw), lru_gate_b=gb, hg_lb=lb,
        hg_norm_g=jnp.tile(p['hg_norm_g'][0], HG_HEADS))


def _trunk(x, mod, p, w):
    b, l, d = x.shape
    e = E_WIDTH
    x = x.astype(F32)
    zero_bias = lambda n: jnp.zeros((n,), F32)

    def split(layer):
        m = mod[layer]
        return m[:, :d], m[:, d:2 * d], m[:, 2 * d:]

    shift, scale, gate = split(0)
    proj = _in_proj(x, p['norm_g'][0], scale, shift, w['hg_w_in'], zero_bias(5 * e))
    o = _hgrn2_dir(proj, w['hg_lb'], False, None)
    o = _hgrn2_dir(proj, w['hg_lb'], True, o)
    x = _out_proj(_mix_hgrn2, [o, proj, w['hg_norm_g'].reshape(1, e)],
                  lambda tm: [_row_spec(tm, e, 0), _row_spec(tm, e, 4), _vec_spec(e)],
                  w['hg_w_out'], x, gate)

    shift, scale, gate = split(1)
    proj = _in_proj(x, p['norm_g'][1], scale, shift, w['hy_w_in'], p['hy_b_in'][0], PROJ_DTYPE_NARROW)
    u, g1 = _hy_pre(proj, p['hy_conv_w'][0], p['hy_conv_b'][0])
    tables = _fft_tables(l)
    sd = _hy_filters(l, p['hy_f_w1'][0], p['hy_f_b1'][0], p['hy_f_w2'][0], p['hy_f_b2'][0],
                     w['hy_f_wout'], p['hy_f_freq'][0])
    spec = _hy_spectrum(sd, tables)
    yc = _hy_conv(u, spec, tables)
    x = _out_proj(_mix_hyena, [yc, u, g1, p['hy_skip'][0].reshape(1, e)],
                  lambda tm: [_row_spec(tm, e, 0)] * 3 + [_vec_spec(e)],
                  w['hy_w_out'], x, gate)

    shift, scale, gate = split(2)
    proj = _in_proj(x, p['norm_g'][2], scale, shift, w['rt_w_in'], zero_bias(2 * RT_QK + 2 * e),
                    PROJ_DTYPE_NARROW)
    cos, sin = _rope_tables(l)
    o = _ret_dir(proj, cos, sin, False, None)
    o = _ret_dir(proj, cos, sin, True, o)
    x = _out_proj(_mix_retention, [o, proj, p['rt_gn_g'][0].reshape(1, e)],
                  lambda tm: [_row_spec(tm, e, 0), _row_spec(tm, e, 2), _vec_spec(e)],
                  w['rt_w_out'], x, gate)

    shift, scale, gate = split(3)
    proj = _in_proj(x, p['norm_g'][3], scale, shift, w['lru_w_in'], zero_bias(2 * e))
    y = None
    for dirn in range(2):
        y = _lru_dir(proj, p['lru_conv_w'][0], p['lru_conv_b'][0], w['lru_gate_w'][dirn],
                     w['lru_gate_b'][dirn], p['lru_lambda'][0][dirn], dirn == 1, y)
    return _out_proj(_mix_lru, [y, proj], lambda tm: [_row_spec(tm, e, 0), _row_spec(tm, e, 1)],
                     w['lru_w_out'], x, gate, final_g=p['final_g'])


def kernel(x_prompt, x_sample, c_prompt, c_sample, ada_w, ada_b, norm_g, final_g, hg_lb, hg_w_in, hg_norm_g, hg_w_out, hy_w_in, hy_b_in, hy_conv_w, hy_conv_b, hy_f_w1, hy_f_b1, hy_f_w2, hy_f_b2, hy_f_wout, hy_f_freq, hy_skip, hy_w_out, rt_w_in, rt_gn_g, rt_w_out, lru_w_in, lru_conv_w, lru_conv_b, lru_gate_w, lru_gate_b, lru_lambda, lru_w_out):
    p = dict(ada_w=ada_w, ada_b=ada_b, norm_g=norm_g, final_g=final_g, hg_lb=hg_lb, hg_w_in=hg_w_in,
             hg_norm_g=hg_norm_g, hg_w_out=hg_w_out, hy_w_in=hy_w_in, hy_b_in=hy_b_in,
             hy_conv_w=hy_conv_w, hy_conv_b=hy_conv_b, hy_f_w1=hy_f_w1, hy_f_b1=hy_f_b1,
             hy_f_w2=hy_f_w2, hy_f_b2=hy_f_b2, hy_f_wout=hy_f_wout, hy_f_freq=hy_f_freq,
             hy_skip=hy_skip, hy_w_out=hy_w_out, rt_w_in=rt_w_in, rt_gn_g=rt_gn_g, rt_w_out=rt_w_out,
             lru_w_in=lru_w_in, lru_conv_w=lru_conv_w, lru_conv_b=lru_conv_b, lru_gate_w=lru_gate_w,
             lru_gate_b=lru_gate_b, lru_lambda=lru_lambda, lru_w_out=lru_w_out)
    w = _prep_weights(p)
    bp, bs = c_prompt.shape[0], c_sample.shape[0]
    rows = -(-(bp + bs) // SUBLANES) * SUBLANES
    c_all = jnp.concatenate([c_prompt, c_sample, jnp.zeros((rows - bp - bs, D_MODEL), F32)], axis=0)
    mod = _adaln(c_all.astype(F32), w['ada_w'], ada_b)
    y_prompt = _trunk(x_prompt, mod[:, :bp], p, w).astype(x_prompt.dtype)
    y_sample = _trunk(x_sample, mod[:, bp:bp + bs], p, w).astype(x_sample.dtype)
    return (y_prompt, y_sample)
```

```python
import functools
import math

import numpy as np
import jax
import jax.numpy as jnp
from jax import lax
from jax.experimental import pallas as pl
from jax.experimental.pallas import tpu as pltpu

F32 = jnp.float32
BF16 = jnp.bfloat16

D_MODEL = 1024
DEPTH = 4
E_WIDTH = 2 * D_MODEL
NORM_EPS = 1e-6
LANES = 128
SUBLANES = 8
MIB = 1024 * 1024
MIX_DTYPE = BF16
PROJ_DTYPE_NARROW = BF16

HG_CHUNK = 64
HG_DK = 128
HG_UNROLL = 8
OUT_PROJ_TM = 512
HG_HEADS = E_WIDTH // HG_DK

HY_EMB = 33
HY_BANDS = 16
HY_FH = 64
HY_INNER = 2
HY_FAST_DECAY = 0.3
HY_SLOW_DECAY = 1.5
HY_TARGET = 1e-2

RT_HEADS = 4
RT_QK = D_MODEL
RT_DK = RT_QK // RT_HEADS
RT_DV = E_WIDTH // RT_HEADS
RT_ROPE_BASE = 10000.0
RT_CHUNK = 256

LRU_CONV = 4
LRU_BLOCKS = 16
LRU_BS = E_WIDTH // LRU_BLOCKS
LRU_C = 8.0

_NT = (((1,), (1,)), ((), ()))


def _cparams(sem, vmem_mib):
    return pltpu.CompilerParams(dimension_semantics=sem, vmem_limit_bytes=vmem_mib * MIB)


def _bdot(a, b):
    return jnp.dot(a.astype(BF16), b.astype(BF16), preferred_element_type=F32)


def _bdot_nt(a, b):
    return lax.dot_general(a.astype(BF16), b.astype(BF16), _NT, preferred_element_type=F32)


def _sigmoid(x):
    return jax.nn.sigmoid(x)


def _silu(x):
    return x * _sigmoid(x)


def _expm1(x):
    u = jnp.exp(x)
    plain = jnp.logical_or(u == 1.0, x < -0.5)
    small = (u - 1.0) * x / jnp.where(plain, 1.0, jnp.log(u))
    return jnp.where(u == 1.0, x, jnp.where(x < -0.5, u - 1.0, small))


def _adaln_kernel(c_ref, w_ref, b_ref, o_ref):
    cs = _silu(c_ref[...])
    o_ref[...] = _bdot(cs, w_ref[...]) + b_ref[...]


def _adaln(c_all, ada_w16, ada_b):
    bp, d = c_all.shape
    tn = 1024
    return pl.pallas_call(
        _adaln_kernel,
        grid=(DEPTH, 3 * d // tn),
        in_specs=[pl.BlockSpec((bp, d), lambda l, j: (0, 0)),
                  pl.BlockSpec((None, d, tn), lambda l, j: (l, 0, j)),
                  pl.BlockSpec((None, 1, tn), lambda l, j: (l, 0, j))],
        out_specs=pl.BlockSpec((None, bp, tn), lambda l, j: (l, 0, j)),
        out_shape=jax.ShapeDtypeStruct((DEPTH, bp, 3 * d), F32),
        compiler_params=_cparams(("arbitrary", "arbitrary"), 32),
        name="adaln",
    )(c_all, ada_w16, ada_b.reshape(DEPTH, 1, 3 * d))


def _in_proj_kernel(x_ref, g_ref, sc_ref, sh_ref, w_ref, b_ref, o_ref, h_ref, *, tm):
    r0 = pl.multiple_of(pl.program_id(2) * tm, tm)

    @pl.when(pl.program_id(1) == 0)
    def _():
        x = x_ref[...]
        ms = jnp.mean(x * x, axis=-1, keepdims=True)
        h = x * lax.rsqrt(ms + NORM_EPS) * g_ref[...] * (1.0 + sc_ref[...]) + sh_ref[...]
        h_ref[pl.ds(r0, tm), :] = h.astype(BF16)

    o_ref[...] = (jnp.dot(h_ref[pl.ds(r0, tm), :], w_ref[...], preferred_element_type=F32)
                  + b_ref[...]).astype(o_ref.dtype)


IN_PROJ_TM = 1024
IN_PROJ_TN = 1024


def _in_proj(x, norm_g, scale, shift, w16, bias, out_dtype=F32):
    b, l, d = x.shape
    p = w16.shape[1]
    tm = min(l, IN_PROJ_TM)
    tn = IN_PROJ_TN
    ni = l // tm
    x_rows = lambda bi, j, i: (bi, jnp.where(j == 0, i, ni - 1), 0)
    return pl.pallas_call(
        functools.partial(_in_proj_kernel, tm=tm),
        grid=(b, p // tn, ni),
        in_specs=[pl.BlockSpec((None, tm, d), x_rows),
                  pl.BlockSpec((1, d), lambda bi, j, i: (0, 0)),
                  pl.BlockSpec((None, 1, d), lambda bi, j, i: (bi, 0, 0)),
                  pl.BlockSpec((None, 1, d), lambda bi, j, i: (bi, 0, 0)),
                  pl.BlockSpec((d, tn), lambda bi, j, i: (0, j)),
                  pl.BlockSpec((1, tn), lambda bi, j, i: (0, j))],
        out_specs=pl.BlockSpec((None, tm, tn), lambda bi, j, i: (bi, i, j)),
        out_shape=jax.ShapeDtypeStruct((b, l, p), out_dtype),
        scratch_shapes=[pltpu.VMEM((l, d), BF16)],
        compiler_params=_cparams(("arbitrary", "arbitrary", "arbitrary"), 48),
        name="in_proj",
    )(x, norm_g.reshape(1, d), scale.reshape(b, 1, d), shift.reshape(b, 1, d), w16,
      bias.reshape(1, p))


def _ld(ref):
    return ref[...].astype(F32)


def _head_rms(o, width):
    parts = []
    for s in range(0, o.shape[1], width):
        oh = o[:, s:s + width]
        ms = jnp.mean(oh * oh, axis=-1, keepdims=True)
        parts.append(oh * lax.rsqrt(ms + NORM_EPS))
    return jnp.concatenate(parts, axis=1)


def _mix_hgrn2(o_ref, z_ref, g_ref):
    return (_head_rms(_ld(o_ref), HG_DK) * g_ref[...]) * _silu(_ld(z_ref))


def _mix_hyena(yc_ref, u_ref, g1_ref, skip_ref):
    return _ld(g1_ref) * (_ld(yc_ref) + _ld(u_ref) * skip_ref[...])


def _mix_retention(o_ref, z_ref, g_ref):
    return (_head_rms(_ld(o_ref), RT_DV) * g_ref[...]) * _silu(_ld(z_ref))


def _mix_lru(y_ref, z_ref):
    return _ld(y_ref) * _silu(_ld(z_ref))


def _out_proj_kernel(*refs, mix, n_mix, final):
    mix_refs = refs[:n_mix]
    w_ref, x_ref, gate_ref = refs[n_mix:n_mix + 3]
    o_ref = refs[-1]
    y = mix(*mix_refs)
    out = x_ref[...] + gate_ref[...] * _bdot(y, w_ref[...])
    if final:
        fg_ref = refs[n_mix + 3]
        ms = jnp.mean(out * out, axis=-1, keepdims=True)
        out = out * lax.rsqrt(ms + NORM_EPS) * fg_ref[...]
    o_ref[...] = out


def _out_proj(mix, mix_args, mix_specs, w16, x, gate, final_g=None):
    b, l, d = x.shape
    e = w16.shape[0]
    tm = min(l, OUT_PROJ_TM)
    in_specs = list(mix_specs(tm)) + [
        pl.BlockSpec((e, d), lambda bi, i: (0, 0)),
        pl.BlockSpec((None, tm, d), lambda bi, i: (bi, i, 0)),
        pl.BlockSpec((None, 1, d), lambda bi, i: (bi, 0, 0))]
    args = list(mix_args) + [w16, x, gate.reshape(b, 1, d)]
    if final_g is not None:
        in_specs.append(pl.BlockSpec((1, d), lambda bi, i: (0, 0)))
        args.append(final_g.reshape(1, d))
    return pl.pallas_call(
        functools.partial(_out_proj_kernel, mix=mix, n_mix=len(mix_args), final=final_g is not None),
        grid=(b, l // tm),
        in_specs=in_specs,
        out_specs=pl.BlockSpec((None, tm, d), lambda bi, i: (bi, i, 0)),
        out_shape=jax.ShapeDtypeStruct((b, l, d), F32),
        compiler_params=_cparams(("arbitrary", "arbitrary"), 48),
        name="out_proj",
    )(*args)


def _row_spec(tm, width, col):
    return pl.BlockSpec((None, tm, width), lambda bi, i: (bi, i, col))


def _vec_spec(width):
    return pl.BlockSpec((1, width), lambda bi, i: (0, 0))


def _cumsum_rows(x, reverse):
    sub = lax.broadcasted_iota(jnp.int32, (SUBLANES, x.shape[1]), 0)
    groups = x.shape[0] // SUBLANES
    order = range(groups - 1, -1, -1) if reverse else range(groups)
    edge = 0 if reverse else SUBLANES - 1
    out = [None] * groups
    total = None
    for gi in order:
        xg = x[gi * SUBLANES:(gi + 1) * SUBLANES, :]
        s = 1
        while s < SUBLANES:
            valid = (sub < SUBLANES - s) if reverse else (sub >= s)
            xg = xg + jnp.where(valid, pltpu.roll(xg, (SUBLANES - s) if reverse else s, 0), 0.0)
            s *= 2
        if total is not None:
            xg = xg + total
        total = xg[edge:edge + 1, :]
        out[gi] = xg
    return jnp.concatenate(out, axis=0)


def _hgrn2_kernel(*refs, reverse, hb, nch, add):
    q_ref, f_ref, v_ref, lb_ref = refs[:4]
    prev_ref = refs[4] if add else None
    o_ref, st_ref = refs[-2:]

    @pl.when(pl.program_id(2) == 0)
    def _():
        st_ref[...] = jnp.zeros_like(st_ref)

    lb_exp = jnp.exp(lb_ref[...] - jnp.max(lb_ref[...], axis=0, keepdims=True))
    lb_all = lb_exp[0:1, :] / jnp.sum(lb_exp, axis=0, keepdims=True)

    c = HG_CHUNK
    row = lax.broadcasted_iota(jnp.int32, (c, c), 0)
    col = lax.broadcasted_iota(jnp.int32, (c, c), 1)
    mask = (col >= row) if reverse else (col <= row)
    mid = c // 2
    ref_row = (c - 1 - mid) if reverse else mid
    last_row = 0 if reverse else c - 1

    def chunk(ci, carry):
        cc = (nch - 1 - ci) if reverse else ci
        r0 = pl.multiple_of(cc * c, c)
        for hh in range(hb):
            sl = slice(hh * HG_DK, (hh + 1) * HG_DK)
            q = _silu(q_ref[pl.ds(r0, c), sl])
            lb = lb_all[:, sl]
            f = lb + (1.0 - lb) * _sigmoid(f_ref[pl.ds(r0, c), sl])
            k = 1.0 - f
            g = jnp.log(f)
            v = v_ref[pl.ds(r0, c), sl]
            bsum = _cumsum_rows(g, reverse)
            b_ref_row = bsum[ref_row:ref_row + 1, :]
            b_last = bsum[last_row:last_row + 1, :]
            scores = _bdot_nt(q * jnp.exp(bsum - b_ref_row), k * jnp.exp(b_ref_row - bsum))
            scores = jnp.where(mask, scores, 0.0)
            st = st_ref[hh]
            vt = v.T
            o = _bdot_nt(jnp.concatenate([q * jnp.exp(bsum), scores], axis=1),
                         jnp.concatenate([st, vt], axis=1))
            st_ref[hh] = st * jnp.exp(b_last) + _bdot(vt, k * jnp.exp(b_last - bsum))
            if add:
                o = o + prev_ref[pl.ds(r0, c), sl].astype(F32)
            o_ref[pl.ds(r0, c), sl] = o.astype(o_ref.dtype)
        return carry

    lax.fori_loop(0, nch, chunk, 0, unroll=HG_UNROLL)


def _hgrn2_dir(proj, lb, reverse, prev):
    b, l, _ = proj.shape
    e = E_WIDTH
    hb = 8
    w = hb * HG_DK
    t = min(l, 512)
    nt = l // t
    ncol = e // w
    fsec = 2 if reverse else 1

    def rows(bi, h, ti):
        return (nt - 1 - ti) if reverse else ti

    def sec(s):
        return pl.BlockSpec((None, t, w), lambda bi, h, ti: (bi, rows(bi, h, ti), s * ncol + h))

    in_specs = [sec(0), sec(fsec), sec(3), pl.BlockSpec((DEPTH + 1, w), lambda bi, h, ti: (0, h))]
    args = [proj, proj, proj, lb]
    out_spec = pl.BlockSpec((None, t, w), lambda bi, h, ti: (bi, rows(bi, h, ti), h))
    if prev is not None:
        in_specs.append(out_spec)
        args.append(prev)
    return pl.pallas_call(
        functools.partial(_hgrn2_kernel, reverse=reverse, hb=hb, nch=t // HG_CHUNK,
                          add=prev is not None),
        grid=(b, ncol, nt),
        in_specs=in_specs,
        out_specs=out_spec,
        out_shape=jax.ShapeDtypeStruct((b, l, e), MIX_DTYPE),
        scratch_shapes=[pltpu.VMEM((hb, HG_DK, HG_DK), F32)],
        compiler_params=_cparams(("arbitrary", "arbitrary", "arbitrary"), 32),
        name="hgrn2_bwd" if reverse else "hgrn2_fwd",
    )(*args)


def _ret_tables(reverse):
    c = RT_CHUNK
    hidx = np.arange(RT_HEADS, dtype=np.float64)
    lg = np.log1p(-np.exp2((-5.5 if reverse else -5.0) - hidx))[:, None]
    pos = np.arange(c, dtype=np.float64)[None, :]
    rel = pos[0][:, None] - pos[0][None, :]
    if reverse:
        rel = -rel
    decay = np.where(rel >= 0, np.exp(lg[:, :, None] * np.maximum(rel, 0.0)[None]), 0.0)
    q_dec = np.exp(lg * ((c - pos) if reverse else (pos + 1.0)))
    k_dec = np.exp(lg * (pos if reverse else (c - 1.0 - pos)))
    c_dec = np.exp(lg * c)
    return (jnp.asarray(decay, F32),
            jnp.asarray(np.broadcast_to(q_dec[:, :, None], (RT_HEADS, c, RT_DV)), F32),
            jnp.asarray(np.broadcast_to(k_dec[:, :, None], (RT_HEADS, c, RT_DK)), F32),
            jnp.asarray(np.broadcast_to(c_dec[:, :, None], (RT_HEADS, 1, RT_DV)), F32))


def _rope_tables(l):
    inv = RT_ROPE_BASE ** (-jnp.arange(0, RT_DK, 2, dtype=F32) / RT_DK)
    ang = jnp.arange(l, dtype=F32)[:, None] * inv[None]
    return jnp.cos(ang), jnp.sin(ang)


def _ret_kernel(*refs, add):
    q_ref, k_ref, v_ref, cos_ref, sin_ref, dec_ref, qd_ref, kd_ref, cd_ref = refs[:9]
    prev_ref = refs[9] if add else None
    o_ref, r_ref = refs[-2:]

    @pl.when(pl.program_id(1) == 0)
    def _():
        r_ref[...] = jnp.zeros_like(r_ref)

    cos = cos_ref[...]
    sin = sin_ref[...]
    half = RT_DK // 2

    def rot(t):
        t1 = t[:, :half]
        t2 = t[:, half:]
        return jnp.concatenate([t1 * cos - t2 * sin, t1 * sin + t2 * cos], axis=1)

    for h in range(RT_HEADS):
        qs = slice(h * RT_DK, (h + 1) * RT_DK)
        vs = slice(h * RT_DV, (h + 1) * RT_DV)
        q = rot(q_ref[:, qs].astype(F32))
        k = rot(k_ref[:, qs].astype(F32)) * (RT_DK ** -0.5)
        v = v_ref[:, vs]
        scores = _bdot_nt(q, k) * dec_ref[h]
        r = r_ref[h]
        o = _bdot(scores, v) + qd_ref[h] * _bdot(q, r)
        r_ref[h] = cd_ref[h] * r + _bdot((k * kd_ref[h]).T, v)
        if add:
            o = o + prev_ref[:, vs].astype(F32)
        o_ref[:, vs] = o.astype(o_ref.dtype)


def _ret_dir(proj, cos, sin, reverse, prev):
    b, l, _ = proj.shape
    c = RT_CHUNK
    nt = l // c
    e = E_WIDTH
    dec, qd, kd, cd = _ret_tables(reverse)

    def rows(ti):
        return (nt - 1 - ti) if reverse else ti

    whole = lambda a: pl.BlockSpec(a.shape, lambda bi, ti: (0, 0, 0))
    in_specs = [
        pl.BlockSpec((None, c, RT_QK), lambda bi, ti: (bi, rows(ti), 0)),
        pl.BlockSpec((None, c, RT_QK), lambda bi, ti: (bi, rows(ti), 1)),
        pl.BlockSpec((None, c, e), lambda bi, ti: (bi, rows(ti), 2 * RT_QK // e)),
        pl.BlockSpec((c, RT_DK // 2), lambda bi, ti: (rows(ti), 0)),
        pl.BlockSpec((c, RT_DK // 2), lambda bi, ti: (rows(ti), 0)),
        whole(dec), whole(qd), whole(kd), whole(cd),
    ]
    args = [proj, proj, proj, cos, sin, dec, qd, kd, cd]
    out_spec = pl.BlockSpec((None, c, e), lambda bi, ti: (bi, rows(ti), 0))
    if prev is not None:
        in_specs.append(out_spec)
        args.append(prev)
    return pl.pallas_call(
        functools.partial(_ret_kernel, add=prev is not None),
        grid=(b, nt),
        in_specs=in_specs,
        out_specs=out_spec,
        out_shape=jax.ShapeDtypeStruct((b, l, e), MIX_DTYPE),
        scratch_shapes=[pltpu.VMEM((RT_HEADS, RT_DK, RT_DV), F32)],
        compiler_params=_cparams(("arbitrary", "arbitrary"), 48),
        name="ret_bwd" if reverse else "ret_fwd",
    )(*args)


def _halo_rows(dtype):
    return SUBLANES * (4 // jnp.dtype(dtype).itemsize)


def _halo_specs(t, w, l, col, order, hr=SUBLANES):
    per = t // hr
    nblk = l // hr
    prev = pl.BlockSpec((None, hr, w),
                        lambda *g: (g[0], jnp.maximum(order(*g) * per - 1, 0), col(*g)))
    nxt = pl.BlockSpec((None, hr, w),
                       lambda *g: (g[0], jnp.minimum((order(*g) + 1) * per, nblk - 1), col(*g)))
    return prev, nxt


def _fill_ext(ext_ref, x_ref, xp_ref, xn_ref, first, last, t):
    hr = xp_ref.shape[0]
    ext_ref[0:SUBLANES, :] = jnp.where(first, 0.0, xp_ref[hr - SUBLANES:hr, :].astype(F32))
    ext_ref[SUBLANES:SUBLANES + t, :] = x_ref[...].astype(F32)
    ext_ref[SUBLANES + t:2 * SUBLANES + t, :] = jnp.where(last, 0.0, xn_ref[0:SUBLANES, :].astype(F32))


def _lru_pitches(t):
    seg = t // SUBLANES
    return seg, seg + 3 * SUBLANES, seg + SUBLANES


def _lru_kernel(*refs, reverse, t, nt, add):
    x_ref, xp_ref, xn_ref, cw_ref, cb_ref, gw_ref, gb_ref, lam_ref = refs[:8]
    prev_ref = refs[8] if add else None
    o_ref, ext_ref, hbuf_ref, carry_ref = refs[-4:]
    ti = pl.program_id(1)
    te = (nt - 1 - ti) if reverse else ti
    seg, pin, pout = _lru_pitches(t)
    nseg = SUBLANES
    left = LRU_CONV // 2

    @pl.when(ti == 0)
    def _():
        carry_ref[...] = jnp.zeros_like(carry_ref)

    neg_lam = -lam_ref[...]
    softplus = jnp.maximum(neg_lam, 0.0) + jnp.log1p(jnp.exp(-jnp.abs(neg_lam)))
    jorder = range(seg - 1, -1, -1) if reverse else range(seg)
    sorder = range(nseg - 1, -1, -1) if reverse else range(nseg)
    edge = 0 if reverse else seg - 1

    for n in range(LRU_BLOCKS):
        sl = slice(n * LRU_BS, (n + 1) * LRU_BS)
        for s in range(nseg):
            r0 = s * seg
            before = (jnp.where(te == 0, 0.0, xp_ref[:, sl]) if s == 0
                      else x_ref[r0 - SUBLANES:r0, sl])
            after = (jnp.where(te == nt - 1, 0.0, xn_ref[:, sl]) if s == nseg - 1
                     else x_ref[r0 + seg:r0 + seg + SUBLANES, sl])
            ext_ref[n, s * pin:s * pin + SUBLANES, :] = before
            ext_ref[n, s * pin + SUBLANES:s * pin + SUBLANES + seg, :] = x_ref[r0:r0 + seg, sl]
            ext_ref[n, s * pin + SUBLANES + seg:s * pin + 2 * SUBLANES + seg, :] = after
        xs = []
        for j in range(seg):
            acc = cb_ref[:, sl]
            for d in range(LRU_CONV):
                acc = acc + cw_ref[d:d + 1, sl] * ext_ref[n, pl.ds(SUBLANES + j + d - left, nseg, stride=pin), :]
            xs.append(acc)
        xn = jnp.concatenate(xs, axis=0)
        gates = _bdot(xn, gw_ref[n]) + gb_ref[n]
        r = _sigmoid(gates[:, :LRU_BS])
        i = _sigmoid(gates[:, LRU_BS:])
        log_a = -LRU_C * r * softplus[:, sl]
        a = jnp.exp(log_a)
        bb = jnp.sqrt(-_expm1(2.0 * log_a)) * (i * xn)
        hs = [None] * seg
        ps = [None] * seg
        h = p = None
        for j in jorder:
            aj = a[j * nseg:(j + 1) * nseg, :]
            bj = bb[j * nseg:(j + 1) * nseg, :]
            h = bj if h is None else aj * h + bj
            p = aj if p is None else aj * p
            hs[j], ps[j] = h, p
        c = carry_ref[:, sl]
        enter = [None] * nseg
        for s in sorder:
            enter[s] = c
            c = hs[edge][s:s + 1, :] + ps[edge][s:s + 1, :] * c
        carry_ref[:, sl] = c
        cin = jnp.concatenate(enter, axis=0)
        for j in range(seg):
            hbuf_ref[n, pl.ds(j, nseg, stride=pout), :] = hs[j] + ps[j] * cin
        for s in range(nseg):
            hn = hbuf_ref[n, s * pout:s * pout + seg, :]
            if add:
                hn = hn + prev_ref[s * seg:(s + 1) * seg, sl].astype(F32)
            o_ref[s * seg:(s + 1) * seg, sl] = hn.astype(o_ref.dtype)


def _lru_dir(proj, conv_w, conv_b, gate_w16, gate_b, lam, reverse, prev):
    b, l, _ = proj.shape
    e = E_WIDTH
    t = min(l, 256)
    nt = l // t

    def order(bi, ti):
        return (nt - 1 - ti) if reverse else ti

    xp_spec, xn_spec = _halo_specs(t, e, l, lambda bi, ti: 0, order)
    in_specs = [
        pl.BlockSpec((None, t, e), lambda bi, ti: (bi, order(bi, ti), 0)), xp_spec, xn_spec,
        pl.BlockSpec((LRU_CONV, e), lambda bi, ti: (0, 0)),
        pl.BlockSpec((1, e), lambda bi, ti: (0, 0)),
        pl.BlockSpec((LRU_BLOCKS, LRU_BS, 2 * LRU_BS), lambda bi, ti: (0, 0, 0)),
        pl.BlockSpec((LRU_BLOCKS, 1, 2 * LRU_BS), lambda bi, ti: (0, 0, 0)),
        pl.BlockSpec((1, e), lambda bi, ti: (0, 0)),
    ]
    args = [proj, proj, proj, conv_w, conv_b.reshape(1, e), gate_w16, gate_b, lam.reshape(1, e)]
    out_spec = pl.BlockSpec((None, t, e), lambda bi, ti: (bi, order(bi, ti), 0))
    if prev is not None:
        in_specs.append(out_spec)
        args.append(prev)
    return pl.pallas_call(
        functools.partial(_lru_kernel, reverse=reverse, t=t, nt=nt, add=prev is not None),
        grid=(b, nt),
        in_specs=in_specs,
        out_specs=out_spec,
        out_shape=jax.ShapeDtypeStruct((b, l, e), MIX_DTYPE),
        scratch_shapes=[pltpu.VMEM((LRU_BLOCKS, SUBLANES * _lru_pitches(t)[1], LRU_BS), F32),
                        pltpu.VMEM((LRU_BLOCKS, SUBLANES * _lru_pitches(t)[2], LRU_BS), F32),
                        pltpu.VMEM((1, e), F32)],
        compiler_params=_cparams(("arbitrary", "arbitrary"), 48),
        name="lru_bwd" if reverse else "lru_fwd",
    )(*args)


def _hy_pre_kernel(x0_ref, x0p_ref, x0n_ref, x1_ref, x1p_ref, x1n_ref, v_ref, vp_ref, vn_ref,
                   z_ref, w0_ref, w1_ref, wv_ref, b0_ref, b1_ref, bv_ref,
                   u_ref, g1_ref, e0_ref, e1_ref, ev_ref, *, t, nt):
    ti = pl.program_id(1)
    first = ti == 0
    last = ti == nt - 1

    def conv(x_ref, xp_ref, xn_ref, ext_ref, w_ref, b_ref):
        _fill_ext(ext_ref, x_ref, xp_ref, xn_ref, first, last, t)
        out = b_ref[...]
        for j in range(3):
            out = out + w_ref[j:j + 1, :] * ext_ref[pl.ds(SUBLANES - 1 + j, t), :]
        return out

    x0 = conv(x0_ref, x0p_ref, x0n_ref, e0_ref, w0_ref, b0_ref)
    x1 = conv(x1_ref, x1p_ref, x1n_ref, e1_ref, w1_ref, b1_ref)
    v = conv(v_ref, vp_ref, vn_ref, ev_ref, wv_ref, bv_ref)
    u_ref[...] = x0 * v
    g1_ref[...] = (x1 * _silu(z_ref[...].astype(F32))).astype(g1_ref.dtype)


def _hy_pre(proj, conv_w, conv_b):
    b, l, _ = proj.shape
    e = E_WIDTH
    w = 512
    t = min(l, 512)
    nt = l // t
    ncol = e // w

    def order(bi, ti, j):
        return ti

    specs, args = [], []
    for s in range(3):
        col = (lambda s: lambda bi, ti, j: s * ncol + j)(s)
        xp, xn = _halo_specs(t, w, l, col, order, _halo_rows(proj.dtype))
        specs += [pl.BlockSpec((None, t, w), (lambda col: lambda bi, ti, j: (bi, ti, col(bi, ti, j)))(col)),
                  xp, xn]
        args += [proj, proj, proj]
    specs.append(pl.BlockSpec((None, t, w), lambda bi, ti, j: (bi, ti, 3 * ncol + j)))
    args.append(proj)
    for s in range(3):
        specs.append(pl.BlockSpec((3, w), (lambda s: lambda bi, ti, j: (0, s * ncol + j))(s)))
        args.append(conv_w)
    cb = conv_b.reshape(1, 3 * e)
    for s in range(3):
        specs.append(pl.BlockSpec((1, w), (lambda s: lambda bi, ti, j: (0, s * ncol + j))(s)))
        args.append(cb)
    out_spec = pl.BlockSpec((None, t, w), lambda bi, ti, j: (bi, ti, j))
    return pl.pallas_call(
        functools.partial(_hy_pre_kernel, t=t, nt=nt),
        grid=(b, nt, ncol),
        in_specs=specs,
        out_specs=[out_spec, out_spec],
        out_shape=[jax.ShapeDtypeStruct((b, l, e), F32), jax.ShapeDtypeStruct((b, l, e), MIX_DTYPE)],
        scratch_shapes=[pltpu.VMEM((t + 2 * SUBLANES, w), F32)] * 3,
        compiler_params=_cparams(("arbitrary", "arbitrary", "arbitrary"), 48),
        name="hy_pre",
    )(*args)


def _hy_filter_kernel(z_ref, w1_ref, b1_ref, w2_ref, b2_ref, fr_ref, wf_ref, wb_ref, dl_ref,
                      sd_ref, a_ref):
    @pl.when(pl.program_id(1) == 0)
    def _():
        fr = fr_ref[...]
        a = jnp.sin(fr * (_bdot(z_ref[...], w1_ref[...]) + b1_ref[...]))
        for j in range(HY_INNER):
            a = jnp.sin(fr * (_bdot(a, w2_ref[j]) + b2_ref[j]))
        a_ref[...] = a

    a = a_ref[...]
    window = jnp.exp(-z_ref[:, 0:1] * dl_ref[...])
    h_fw = _bdot(a, wf_ref[...]) * window
    h_bw = _bdot(a, wb_ref[...]) * window
    sd_ref[0] = h_fw + h_bw
    sd_ref[1] = h_fw - h_bw


def _hy_filters(l, w1, b1, w2, b2, wout16, freq):
    e = E_WIDTH
    kp = LANES
    t = jnp.linspace(0.0, 1.0, l, dtype=F32)[:, None]
    wv = 2.0 * math.pi * jnp.arange(l, dtype=F32)[:, None] / l
    bands = jnp.linspace(1e-4, HY_BANDS - 1, HY_BANDS, dtype=F32)[None]
    z = jnp.concatenate([t, jnp.cos(bands * wv), -jnp.sin(bands * wv),
                         jnp.zeros((l, kp - HY_EMB), F32)], axis=-1)
    w1p = jnp.concatenate([w1, jnp.zeros((kp - HY_EMB, HY_FH), F32)], axis=0)
    max_decay = math.log(HY_TARGET) / HY_FAST_DECAY
    min_decay = math.log(HY_TARGET) / HY_SLOW_DECAY
    deltas = jnp.abs(jnp.linspace(min_decay, max_decay, e, dtype=F32))[None]
    tm = min(l, 512)
    w = 512
    ncol = e // w
    return pl.pallas_call(
        _hy_filter_kernel,
        grid=(l // tm, ncol),
        in_specs=[pl.BlockSpec((tm, kp), lambda i, j: (i, 0)),
                  pl.BlockSpec((kp, HY_FH), lambda i, j: (0, 0)),
                  pl.BlockSpec((1, HY_FH), lambda i, j: (0, 0)),
                  pl.BlockSpec((HY_INNER, HY_FH, HY_FH), lambda i, j: (0, 0, 0)),
                  pl.BlockSpec((HY_INNER, 1, HY_FH), lambda i, j: (0, 0, 0)),
                  pl.BlockSpec((1, HY_FH), lambda i, j: (0, 0)),
                  pl.BlockSpec((HY_FH, w), lambda i, j: (0, j)),
                  pl.BlockSpec((HY_FH, w), lambda i, j: (0, ncol + j)),
                  pl.BlockSpec((1, w), lambda i, j: (0, j))],
        out_specs=pl.BlockSpec((2, tm, w), lambda i, j: (0, i, j)),
        out_shape=jax.ShapeDtypeStruct((2, l, e), F32),
        scratch_shapes=[pltpu.VMEM((tm, HY_FH), F32)],
        compiler_params=_cparams(("arbitrary", "arbitrary"), 32),
        name="hy_filter",
    )(z, w1p, b1.reshape(1, HY_FH), w2, b2.reshape(HY_INNER, 1, HY_FH), freq.reshape(1, HY_FH),
      wout16, wout16, deltas)


FFT_UNROLL = 16


def _fft_dims(l):
    n = 2 * l
    n1 = int(round(math.sqrt(n)))
    assert n1 * n1 == n and n1 % 16 == 0, "sequence length must give a square DFT factorisation"
    return n1, n1


def _fft_pitch(n1):
    return 2 * n1 + SUBLANES


def _fft_tables(l):
    n1, n2 = _fft_dims(l)
    n = n1 * n2
    k1 = np.arange(n1)[:, None]
    m1 = np.arange(n1 // 2)[None, :]
    j2 = np.arange(n2)[:, None, None]
    ang = -2.0 * np.pi * (k1 * m1 / n1)[None] - 2.0 * np.pi * (j2 * k1[None] / n)
    gr, gi = np.cos(ang), np.sin(ang)
    g_fwd = np.concatenate([gr, gi], axis=1)
    g_fwd2 = np.concatenate([np.concatenate([gr, -gi], axis=2), np.concatenate([gi, gr], axis=2)], axis=1)
    ang_i = 2.0 * np.pi * (m1.T * k1.T / n1)[None] + 2.0 * np.pi * (j2 * k1.T[None] / n)
    er, ei = np.cos(ang_i) / n, np.sin(ang_i) / n
    g_inv2 = np.concatenate([np.concatenate([er, -ei], axis=2), np.concatenate([ei, er], axis=2)], axis=1)
    a2 = -2.0 * np.pi * np.arange(n2)[:, None] * np.arange(n2)[None, :] / n2
    fr, fi = np.cos(a2), np.sin(a2)
    f2 = np.block([[fr, -fi], [fi, fr]])
    f2_inv = np.block([[fr, fi], [-fi, fr]])
    f2_half = np.stack([np.concatenate([fr, -fi], axis=1), np.concatenate([fi, fr], axis=1)])
    as16 = lambda a: jnp.asarray(a, F32).astype(BF16)
    return dict(g_fwd=as16(g_fwd), g_fwd2=as16(g_fwd2), g_inv2=as16(g_inv2), f2=as16(f2),
                f2_inv=as16(f2_inv), f2_half=as16(f2_half))


def _fft_stage1(x_ref, g_ref, work_ref, n2_lo, cnt, n1, n2, pitch):
    def body(j, carry):
        jj = n2_lo + j
        xs = x_ref[pl.ds(jj, n1 // 2, stride=n2), :]
        r0 = pl.multiple_of(jj * pitch, SUBLANES)
        work_ref[pl.ds(r0, 2 * n1), :] = _bdot(g_ref[j], xs)
        return carry
    lax.fori_loop(0, cnt, body, 0, unroll=FFT_UNROLL)


def _fft_load_k1(work_ref, k1, n1, n2, pitch):
    br = work_ref[pl.ds(k1, n2, stride=pitch), :]
    bi = work_ref[pl.ds(n1 + k1, n2, stride=pitch), :]
    return jnp.concatenate([br, bi], axis=0)


def _hy_spec_kernel(x_ref, g_ref, f2_ref, t_ref, work_ref, *, n1, n2, nc):
    p = pl.program_id(2)
    pitch = _fft_pitch(n1)
    c2 = n2 // nc
    c1 = n1 // nc

    @pl.when(p < nc)
    def _():
        _fft_stage1(x_ref, g_ref, work_ref, p * c2, c2, n1, n2, pitch)

    @pl.when(p >= nc)
    def _():
        def body(j, carry):
            k1 = (p - nc) * c1 + j
            t_ref[j] = _bdot(f2_ref[...], _fft_load_k1(work_ref, k1, n1, n2, pitch))
            return carry
        lax.fori_loop(0, c1, body, 0, unroll=FFT_UNROLL)


def _fft_nc(l):
    return 8 if l >= 8192 else (2 if l >= 2048 else 1)


def _hy_spectrum(sd, tables):
    _, l, e = sd.shape
    n1, n2 = _fft_dims(l)
    nc = _fft_nc(l)
    g_fwd, f2_half = tables['g_fwd'], tables['f2_half']
    pitch = _fft_pitch(n1)
    return pl.pallas_call(
        functools.partial(_hy_spec_kernel, n1=n1, n2=n2, nc=nc),
        grid=(e // LANES, 2, 2 * nc),
        in_specs=[pl.BlockSpec((None, l, LANES), lambda c, j, p: (j, 0, c)),
                  pl.BlockSpec((n2 // nc, 2 * n1, n1 // 2), lambda c, j, p: (jnp.minimum(p, nc - 1), 0, 0)),
                  pl.BlockSpec((None, n2, 2 * n2), lambda c, j, p: (j, 0, 0))],
        out_specs=pl.BlockSpec((n1 // nc, n2, LANES),
                               lambda c, j, p: (jnp.maximum(p - nc, 0), j, c)),
        out_shape=jax.ShapeDtypeStruct((n1, 2 * n2, e), F32),
        scratch_shapes=[pltpu.VMEM((n2 * pitch, LANES), F32)],
        compiler_params=_cparams(("arbitrary", "arbitrary", "arbitrary"), 48),
        name="hy_spectrum",
    )(sd, g_fwd, f2_half)


def _hy_conv_kernel(u_ref, gf_ref, f2_ref, f2i_ref, t_ref, gi_ref, y_ref, work_ref, *, n1, n2, nc):
    p = pl.program_id(2)
    pitch = _fft_pitch(n1)
    c2 = n2 // nc
    c1 = n1 // nc

    @pl.when(p < nc)
    def _():
        def body(j, carry):
            jj = p * c2 + j
            xs = jnp.concatenate([u_ref[0, pl.ds(jj, n1 // 2, stride=n2), :],
                                  u_ref[1, pl.ds(jj, n1 // 2, stride=n2), :]], axis=0)
            r0 = pl.multiple_of(jj * pitch, SUBLANES)
            work_ref[pl.ds(r0, 2 * n1), :] = _bdot(gf_ref[j], xs)
            return carry
        lax.fori_loop(0, c2, body, 0, unroll=FFT_UNROLL)

    @pl.when(jnp.logical_and(p >= nc, p < 2 * nc))
    def _():
        def body(j, carry):
            k1 = (p - nc) * c1 + j
            x = _bdot(f2_ref[...], _fft_load_k1(work_ref, k1, n1, n2, pitch))
            xr, xi = x[:n2], x[n2:]
            tr, ti = t_ref[j, :n2, :], t_ref[j, n2:, :]
            z = jnp.concatenate([xr * tr - xi * ti, xr * ti + xi * tr], axis=0)
            cmat = _bdot(f2i_ref[...], z)
            work_ref[pl.ds(k1, n2, stride=pitch), :] = cmat[:n2]
            work_ref[pl.ds(n1 + k1, n2, stride=pitch), :] = cmat[n2:]
            return carry
        lax.fori_loop(0, c1, body, 0, unroll=FFT_UNROLL)

    @pl.when(p >= 2 * nc)
    def _():
        def body(j, carry):
            jj = (p - 2 * nc) * c2 + j
            r0 = pl.multiple_of(jj * pitch, SUBLANES)
            d = work_ref[pl.ds(r0, 2 * n1), :]
            y = _bdot(gi_ref[j], d)
            y_ref[0, pl.ds(jj, n1 // 2, stride=n2), :] = y[:n1 // 2]
            y_ref[1, pl.ds(jj, n1 // 2, stride=n2), :] = y[n1 // 2:]
            return carry
        lax.fori_loop(0, c2, body, 0, unroll=FFT_UNROLL)


def _hy_conv(u, spec, tables):
    b, l, e = u.shape
    assert b % 2 == 0, "batch rows are transformed in pairs"
    n1, n2 = _fft_dims(l)
    nc = _fft_nc(l)
    pitch = _fft_pitch(n1)
    clip = lambda v: jnp.clip(v, 0, nc - 1)
    pair_spec = lambda **kw: pl.BlockSpec((2, l, LANES), lambda c, bi, p: (bi, 0, c), **kw)
    return pl.pallas_call(
        functools.partial(_hy_conv_kernel, n1=n1, n2=n2, nc=nc),
        grid=(e // LANES, b // 2, 3 * nc),
        in_specs=[pair_spec(),
                  pl.BlockSpec((n2 // nc, 2 * n1, n1), lambda c, bi, p: (clip(p), 0, 0)),
                  pl.BlockSpec((2 * n2, 2 * n2), lambda c, bi, p: (0, 0)),
                  pl.BlockSpec((2 * n2, 2 * n2), lambda c, bi, p: (0, 0)),
                  pl.BlockSpec((n1 // nc, 2 * n2, LANES), lambda c, bi, p: (clip(p - nc), 0, c)),
                  pl.BlockSpec((n2 // nc, n1, 2 * n1), lambda c, bi, p: (clip(p - 2 * nc), 0, 0))],
        out_specs=pair_spec(pipeline_mode=pl.Buffered(1)),
        out_shape=jax.ShapeDtypeStruct((b, l, e), F32),
        scratch_shapes=[pltpu.VMEM((n2 * pitch, LANES), F32)],
        compiler_params=_cparams(("arbitrary", "arbitrary", "arbitrary"), 56),
        name="hy_conv",
    )(u, tables['g_fwd2'], tables['f2'], tables['f2_inv'], spec, tables['g_inv2'])


def _prep_weights(p):
    c16 = lambda a: a.astype(BF16)
    gw = p['lru_gate_w'][0]
    gw = jnp.concatenate([gw[:, 0], gw[:, 1]], axis=-1)
    gb = p['lru_gate_b'][0].reshape(2, 2, LRU_BLOCKS, 1, LRU_BS)
    gb = jnp.concatenate([gb[:, 0], gb[:, 1]], axis=-1)
    lb = p['hg_lb'].astype(F32)
    return dict(
        ada_w=c16(p['ada_w']), hg_w_in=c16(p['hg_w_in'][0]), hg_w_out=c16(p['hg_w_out'][0]),
        hy_w_in=c16(p['hy_w_in'][0]), hy_w_out=c16(p['hy_w_out'][0]), hy_f_wout=c16(p['hy_f_wout'][0]),
        rt_w_in=c16(p['rt_w_in'][0]), rt_w_out=c16(p['rt_w_out'][0]),
        lru_w_in=c16(p['lru_w_in'][0]), lru_w_out=c16(p['lru_w_out'][0]),
        lru_gate_w=c16(gw), lru_gate_b=gb, hg_lb=lb,
        hg_norm_g=jnp.tile(p['hg_norm_g'][0], HG_HEADS))


def _trunk(x, mod, p, w):
    b, l, d = x.shape
    e = E_WIDTH
    x = x.astype(F32)
    zero_bias = lambda n: jnp.zeros((n,), F32)

    def split(layer):
        m = mod[layer]
        return m[:, :d], m[:, d:2 * d], m[:, 2 * d:]

    shift, scale, gate = split(0)
    proj = _in_proj(x, p['norm_g'][0], scale, shift, w['hg_w_in'], zero_bias(5 * e))
    o = _hgrn2_dir(proj, w['hg_lb'], False, None)
    o = _hgrn2_dir(proj, w['hg_lb'], True, o)
    x = _out_proj(_mix_hgrn2, [o, proj, w['hg_norm_g'].reshape(1, e)],
                  lambda tm: [_row_spec(tm, e, 0), _row_spec(tm, e, 4), _vec_spec(e)],
                  w['hg_w_out'], x, gate)

    shift, scale, gate = split(1)
    proj = _in_proj(x, p['norm_g'][1], scale, shift, w['hy_w_in'], p['hy_b_in'][0], PROJ_DTYPE_NARROW)
    u, g1 = _hy_pre(proj, p['hy_conv_w'][0], p['hy_conv_b'][0])
    tables = _fft_tables(l)
    sd = _hy_filters(l, p['hy_f_w1'][0], p['hy_f_b1'][0], p['hy_f_w2'][0], p['hy_f_b2'][0],
                     w['hy_f_wout'], p['hy_f_freq'][0])
    spec = _hy_spectrum(sd, tables)
    yc = _hy_conv(u, spec, tables)
    x = _out_proj(_mix_hyena, [yc, u, g1, p['hy_skip'][0].reshape(1, e)],
                  lambda tm: [_row_spec(tm, e, 0)] * 3 + [_vec_spec(e)],
                  w['hy_w_out'], x, gate)

    shift, scale, gate = split(2)
    proj = _in_proj(x, p['norm_g'][2], scale, shift, w['rt_w_in'], zero_bias(2 * RT_QK + 2 * e),
                    PROJ_DTYPE_NARROW)
    cos, sin = _rope_tables(l)
    o = _ret_dir(proj, cos, sin, False, None)
    o = _ret_dir(proj, cos, sin, True, o)
    x = _out_proj(_mix_retention, [o, proj, p['rt_gn_g'][0].reshape(1, e)],
                  lambda tm: [_row_spec(tm, e, 0), _row_spec(tm, e, 2), _vec_spec(e)],
                  w['rt_w_out'], x, gate)

    shift, scale, gate = split(3)
    proj = _in_proj(x, p['norm_g'][3], scale, shift, w['lru_w_in'], zero_bias(2 * e))
    y = None
    for dirn in range(2):
        y = _lru_dir(proj, p['lru_conv_w'][0], p['lru_conv_b'][0], w['lru_gate_w'][dirn],
                     w['lru_gate_b'][dirn], p['lru_lambda'][0][dirn], dirn == 1, y)
    return _out_proj(_mix_lru, [y, proj], lambda tm: [_row_spec(tm, e, 0), _row_spec(tm, e, 1)],
                     w['lru_w_out'], x, gate, final_g=p['final_g'])


def kernel(x_prompt, x_sample, c_prompt, c_sample, ada_w, ada_b, norm_g, final_g, hg_lb, hg_w_in, hg_norm_g, hg_w_out, hy_w_in, hy_b_in, hy_conv_w, hy_conv_b, hy_f_w1, hy_f_b1, hy_f_w2, hy_f_b2, hy_f_wout, hy_f_freq, hy_skip, hy_w_out, rt_w_in, rt_gn_g, rt_w_out, lru_w_in, lru_conv_w, lru_conv_b, lru_gate_w, lru_gate_b, lru_lambda, lru_w_out):
    p = dict(ada_w=ada_w, ada_b=ada_b, norm_g=norm_g, final_g=final_g, hg_lb=hg_lb, hg_w_in=hg_w_in,
             hg_norm_g=hg_norm_g, hg_w_out=hg_w_out, hy_w_in=hy_w_in, hy_b_in=hy_b_in,
             hy_conv_w=hy_conv_w, hy_conv_b=hy_conv_b, hy_f_w1=hy_f_w1, hy_f_b1=hy_f_b1,
             hy_f_w2=hy_f_w2, hy_f_b2=hy_f_b2, hy_f_wout=hy_f_wout, hy_f_freq=hy_f_freq,
             hy_skip=hy_skip, hy_w_out=hy_w_out, rt_w_in=rt_w_in, rt_gn_g=rt_gn_g, rt_w_out=rt_w_out,
             lru_w_in=lru_w_in, lru_conv_w=lru_conv_w, lru_conv_b=lru_conv_b, lru_gate_w=lru_gate_w,
             lru_gate_b=lru_gate_b, lru_lambda=lru_lambda, lru_w_out=lru_w_out)
    w = _prep_weights(p)
    bp, bs = c_prompt.shape[0], c_sample.shape[0]
    rows = -(-(bp + bs) // SUBLANES) * SUBLANES
    c_all = jnp.concatenate([c_prompt, c_sample, jnp.zeros((rows - bp - bs, D_MODEL), F32)], axis=0)
    mod = _adaln(c_all.astype(F32), w['ada_w'], ada_b)
    y_prompt = _trunk(x_prompt, mod[:, :bp], p, w).astype(x_prompt.dtype)
    y_sample = _trunk(x_sample, mod[:, bp:bp + bs], p, w).astype(x_sample.dtype)
    return (y_prompt, y_sample)
```

```python
import functools
import math

import numpy as np
import jax
import jax.numpy as jnp
from jax import lax
from jax.experimental import pallas as pl
from jax.experimental.pallas import tpu as pltpu

F32 = jnp.float32
BF16 = jnp.bfloat16

D_MODEL = 1024
DEPTH = 4
E_WIDTH = 2 * D_MODEL
NORM_EPS = 1e-6
LANES = 128
SUBLANES = 8
MIB = 1024 * 1024
MIX_DTYPE = BF16
PROJ_DTYPE_NARROW = BF16

HG_CHUNK = 64
HG_DK = 128
HG_UNROLL = 4
OUT_PROJ_TM = 512
HG_HEADS = E_WIDTH // HG_DK

HY_EMB = 33
HY_BANDS = 16
HY_FH = 64
HY_INNER = 2
HY_FAST_DECAY = 0.3
HY_SLOW_DECAY = 1.5
HY_TARGET = 1e-2

RT_HEADS = 4
RT_QK = D_MODEL
RT_DK = RT_QK // RT_HEADS
RT_DV = E_WIDTH // RT_HEADS
RT_ROPE_BASE = 10000.0
RT_CHUNK = 256

LRU_CONV = 4
LRU_BLOCKS = 16
LRU_BS = E_WIDTH // LRU_BLOCKS
LRU_C = 8.0

_NT = (((1,), (1,)), ((), ()))


def _cparams(sem, vmem_mib):
    return pltpu.CompilerParams(dimension_semantics=sem, vmem_limit_bytes=vmem_mib * MIB)


def _bdot(a, b):
    return jnp.dot(a.astype(BF16), b.astype(BF16), preferred_element_type=F32)


def _bdot_nt(a, b):
    return lax.dot_general(a.astype(BF16), b.astype(BF16), _NT, preferred_element_type=F32)


def _sigmoid(x):
    return jax.nn.sigmoid(x)


def _silu(x):
    return x * _sigmoid(x)


def _expm1(x):
    u = jnp.exp(x)
    plain = jnp.logical_or(u == 1.0, x < -0.5)
    small = (u - 1.0) * x / jnp.where(plain, 1.0, jnp.log(u))
    return jnp.where(u == 1.0, x, jnp.where(x < -0.5, u - 1.0, small))


def _adaln_kernel(c_ref, w_ref, b_ref, o_ref):
    cs = _silu(c_ref[...])
    o_ref[...] = _bdot(cs, w_ref[...]) + b_ref[...]


def _adaln(c_all, ada_w16, ada_b):
    bp, d = c_all.shape
    tn = 1024
    return pl.pallas_call(
        _adaln_kernel,
        grid=(DEPTH, 3 * d // tn),
        in_specs=[pl.BlockSpec((bp, d), lambda l, j: (0, 0)),
                  pl.BlockSpec((None, d, tn), lambda l, j: (l, 0, j)),
                  pl.BlockSpec((None, 1, tn), lambda l, j: (l, 0, j))],
        out_specs=pl.BlockSpec((None, bp, tn), lambda l, j: (l, 0, j)),
        out_shape=jax.ShapeDtypeStruct((DEPTH, bp, 3 * d), F32),
        compiler_params=_cparams(("arbitrary", "arbitrary"), 32),
        name="adaln",
    )(c_all, ada_w16, ada_b.reshape(DEPTH, 1, 3 * d))


def _in_proj_kernel(x_ref, g_ref, sc_ref, sh_ref, w_ref, b_ref, o_ref, h_ref, *, tm):
    r0 = pl.multiple_of(pl.program_id(2) * tm, tm)

    @pl.when(pl.program_id(1) == 0)
    def _():
        x = x_ref[...]
        ms = jnp.mean(x * x, axis=-1, keepdims=True)
        h = x * lax.rsqrt(ms + NORM_EPS) * g_ref[...] * (1.0 + sc_ref[...]) + sh_ref[...]
        h_ref[pl.ds(r0, tm), :] = h.astype(BF16)

    o_ref[...] = (jnp.dot(h_ref[pl.ds(r0, tm), :], w_ref[...], preferred_element_type=F32)
                  + b_ref[...]).astype(o_ref.dtype)


IN_PROJ_TM = 1024
IN_PROJ_TN = 1024


def _in_proj(x, norm_g, scale, shift, w16, bias, out_dtype=F32):
    b, l, d = x.shape
    p = w16.shape[1]
    tm = min(l, IN_PROJ_TM)
    tn = IN_PROJ_TN
    ni = l // tm
    x_rows = lambda bi, j, i: (bi, jnp.where(j == 0, i, ni - 1), 0)
    return pl.pallas_call(
        functools.partial(_in_proj_kernel, tm=tm),
        grid=(b, p // tn, ni),
        in_specs=[pl.BlockSpec((None, tm, d), x_rows),
                  pl.BlockSpec((1, d), lambda bi, j, i: (0, 0)),
                  pl.BlockSpec((None, 1, d), lambda bi, j, i: (bi, 0, 0)),
                  pl.BlockSpec((None, 1, d), lambda bi, j, i: (bi, 0, 0)),
                  pl.BlockSpec((d, tn), lambda bi, j, i: (0, j)),
                  pl.BlockSpec((1, tn), lambda bi, j, i: (0, j))],
        out_specs=pl.BlockSpec((None, tm, tn), lambda bi, j, i: (bi, i, j)),
        out_shape=jax.ShapeDtypeStruct((b, l, p), out_dtype),
        scratch_shapes=[pltpu.VMEM((l, d), BF16)],
        compiler_params=_cparams(("arbitrary", "arbitrary", "arbitrary"), 48),
        name="in_proj",
    )(x, norm_g.reshape(1, d), scale.reshape(b, 1, d), shift.reshape(b, 1, d), w16,
      bias.reshape(1, p))


def _ld(ref):
    return ref[...].astype(F32)


def _head_rms(o, width):
    parts = []
    for s in range(0, o.shape[1], width):
        oh = o[:, s:s + width]
        ms = jnp.mean(oh * oh, axis=-1, keepdims=True)
        parts.append(oh * lax.rsqrt(ms + NORM_EPS))
    return jnp.concatenate(parts, axis=1)


def _mix_hgrn2(o_ref, z_ref, g_ref):
    return (_head_rms(_ld(o_ref), HG_DK) * g_ref[...]) * _silu(_ld(z_ref))


def _mix_hyena(yc_ref, u_ref, g1_ref, skip_ref):
    return _ld(g1_ref) * (_ld(yc_ref) + _ld(u_ref) * skip_ref[...])


def _mix_retention(o_ref, z_ref, g_ref):
    return (_head_rms(_ld(o_ref), RT_DV) * g_ref[...]) * _silu(_ld(z_ref))


def _mix_lru(y_ref, z_ref):
    return _ld(y_ref) * _silu(_ld(z_ref))


def _out_proj_kernel(*refs, mix, n_mix, final):
    mix_refs = refs[:n_mix]
    w_ref, x_ref, gate_ref = refs[n_mix:n_mix + 3]
    o_ref = refs[-1]
    y = mix(*mix_refs)
    out = x_ref[...] + gate_ref[...] * _bdot(y, w_ref[...])
    if final:
        fg_ref = refs[n_mix + 3]
        ms = jnp.mean(out * out, axis=-1, keepdims=True)
        out = out * lax.rsqrt(ms + NORM_EPS) * fg_ref[...]
    o_ref[...] = out


def _out_proj(mix, mix_args, mix_specs, w16, x, gate, final_g=None):
    b, l, d = x.shape
    e = w16.shape[0]
    tm = min(l, OUT_PROJ_TM)
    in_specs = list(mix_specs(tm)) + [
        pl.BlockSpec((e, d), lambda bi, i: (0, 0)),
        pl.BlockSpec((None, tm, d), lambda bi, i: (bi, i, 0)),
        pl.BlockSpec((None, 1, d), lambda bi, i: (bi, 0, 0))]
    args = list(mix_args) + [w16, x, gate.reshape(b, 1, d)]
    if final_g is not None:
        in_specs.append(pl.BlockSpec((1, d), lambda bi, i: (0, 0)))
        args.append(final_g.reshape(1, d))
    return pl.pallas_call(
        functools.partial(_out_proj_kernel, mix=mix, n_mix=len(mix_args), final=final_g is not None),
        grid=(b, l // tm),
        in_specs=in_specs,
        out_specs=pl.BlockSpec((None, tm, d), lambda bi, i: (bi, i, 0)),
        out_shape=jax.ShapeDtypeStruct((b, l, d), F32),
        compiler_params=_cparams(("arbitrary", "arbitrary"), 48),
        name="out_proj",
    )(*args)


def _row_spec(tm, width, col):
    return pl.BlockSpec((None, tm, width), lambda bi, i: (bi, i, col))


def _vec_spec(width):
    return pl.BlockSpec((1, width), lambda bi, i: (0, 0))


def _cumsum_rows(x, reverse):
    sub = lax.broadcasted_iota(jnp.int32, (SUBLANES, x.shape[1]), 0)
    groups = x.shape[0] // SUBLANES
    order = range(groups - 1, -1, -1) if reverse else range(groups)
    edge = 0 if reverse else SUBLANES - 1
    out = [None] * groups
    total = None
    for gi in order:
        xg = x[gi * SUBLANES:(gi + 1) * SUBLANES, :]
        s = 1
        while s < SUBLANES:
            valid = (sub < SUBLANES - s) if reverse else (sub >= s)
            xg = xg + jnp.where(valid, pltpu.roll(xg, (SUBLANES - s) if reverse else s, 0), 0.0)
            s *= 2
        if total is not None:
            xg = xg + total
        total = xg[edge:edge + 1, :]
        out[gi] = xg
    return jnp.concatenate(out, axis=0)


def _hgrn2_kernel(*refs, reverse, hb, nch, add):
    q_ref, f_ref, v_ref, lb_ref = refs[:4]
    prev_ref = refs[4] if add else None
    o_ref, st_ref = refs[-2:]

    @pl.when(pl.program_id(2) == 0)
    def _():
        st_ref[...] = jnp.zeros_like(st_ref)

    lb_exp = jnp.exp(lb_ref[...] - jnp.max(lb_ref[...], axis=0, keepdims=True))
    lb_all = lb_exp[0:1, :] / jnp.sum(lb_exp, axis=0, keepdims=True)

    c = HG_CHUNK
    row = lax.broadcasted_iota(jnp.int32, (c, c), 0)
    col = lax.broadcasted_iota(jnp.int32, (c, c), 1)
    mask = (col >= row) if reverse else (col <= row)
    mid = c // 2
    ref_row = (c - 1 - mid) if reverse else mid
    last_row = 0 if reverse else c - 1

    def chunk(ci, carry):
        cc = (nch - 1 - ci) if reverse else ci
        r0 = pl.multiple_of(cc * c, c)
        for hh in range(hb):
            sl = slice(hh * HG_DK, (hh + 1) * HG_DK)
            q = _silu(q_ref[pl.ds(r0, c), sl])
            lb = lb_all[:, sl]
            f = lb + (1.0 - lb) * _sigmoid(f_ref[pl.ds(r0, c), sl])
            k = 1.0 - f
            g = jnp.log(f)
            v = v_ref[pl.ds(r0, c), sl]
            bsum = _cumsum_rows(g, reverse)
            b_ref_row = bsum[ref_row:ref_row + 1, :]
            b_last = bsum[last_row:last_row + 1, :]
            scores = _bdot_nt(q * jnp.exp(bsum - b_ref_row), k * jnp.exp(b_ref_row - bsum))
            scores = jnp.where(mask, scores, 0.0)
            st = st_ref[hh]
            vt = v.T
            o = _bdot_nt(jnp.concatenate([q * jnp.exp(bsum), scores], axis=1),
                         jnp.concatenate([st, vt], axis=1))
            st_ref[hh] = st * jnp.exp(b_last) + _bdot(vt, k * jnp.exp(b_last - bsum))
            if add:
                o = o + prev_ref[pl.ds(r0, c), sl].astype(F32)
            o_ref[pl.ds(r0, c), sl] = o.astype(o_ref.dtype)
        return carry

    lax.fori_loop(0, nch, chunk, 0, unroll=HG_UNROLL)


def _hgrn2_dir(proj, lb, reverse, prev):
    b, l, _ = proj.shape
    e = E_WIDTH
    hb = 8
    w = hb * HG_DK
    t = min(l, 512)
    nt = l // t
    ncol = e // w
    fsec = 2 if reverse else 1

    def rows(bi, h, ti):
        return (nt - 1 - ti) if reverse else ti

    def sec(s):
        return pl.BlockSpec((None, t, w), lambda bi, h, ti: (bi, rows(bi, h, ti), s * ncol + h))

    in_specs = [sec(0), sec(fsec), sec(3), pl.BlockSpec((DEPTH + 1, w), lambda bi, h, ti: (0, h))]
    args = [proj, proj, proj, lb]
    out_spec = pl.BlockSpec((None, t, w), lambda bi, h, ti: (bi, rows(bi, h, ti), h))
    if prev is not None:
        in_specs.append(out_spec)
        args.append(prev)
    return pl.pallas_call(
        functools.partial(_hgrn2_kernel, reverse=reverse, hb=hb, nch=t // HG_CHUNK,
                          add=prev is not None),
        grid=(b, ncol, nt),
        in_specs=in_specs,
        out_specs=out_spec,
        out_shape=jax.ShapeDtypeStruct((b, l, e), MIX_DTYPE),
        scratch_shapes=[pltpu.VMEM((hb, HG_DK, HG_DK), F32)],
        compiler_params=_cparams(("arbitrary", "arbitrary", "arbitrary"), 32),
        name="hgrn2_bwd" if reverse else "hgrn2_fwd",
    )(*args)


def _ret_tables(reverse):
    c = RT_CHUNK
    hidx = np.arange(RT_HEADS, dtype=np.float64)
    lg = np.log1p(-np.exp2((-5.5 if reverse else -5.0) - hidx))[:, None]
    pos = np.arange(c, dtype=np.float64)[None, :]
    rel = pos[0][:, None] - pos[0][None, :]
    if reverse:
        rel = -rel
    decay = np.where(rel >= 0, np.exp(lg[:, :, None] * np.maximum(rel, 0.0)[None]), 0.0)
    q_dec = np.exp(lg * ((c - pos) if reverse else (pos + 1.0)))
    k_dec = np.exp(lg * (pos if reverse else (c - 1.0 - pos)))
    c_dec = np.exp(lg * c)
    return (jnp.asarray(decay, F32),
            jnp.asarray(np.broadcast_to(q_dec[:, :, None], (RT_HEADS, c, RT_DV)), F32),
            jnp.asarray(np.broadcast_to(k_dec[:, :, None], (RT_HEADS, c, RT_DK)), F32),
            jnp.asarray(np.broadcast_to(c_dec[:, :, None], (RT_HEADS, 1, RT_DV)), F32))


def _rope_tables(l):
    inv = RT_ROPE_BASE ** (-jnp.arange(0, RT_DK, 2, dtype=F32) / RT_DK)
    ang = jnp.arange(l, dtype=F32)[:, None] * inv[None]
    return jnp.cos(ang), jnp.sin(ang)


def _ret_kernel(*refs, add):
    q_ref, k_ref, v_ref, cos_ref, sin_ref, dec_ref, qd_ref, kd_ref, cd_ref = refs[:9]
    prev_ref = refs[9] if add else None
    o_ref, r_ref = refs[-2:]

    @pl.when(pl.program_id(1) == 0)
    def _():
        r_ref[...] = jnp.zeros_like(r_ref)

    cos = cos_ref[...]
    sin = sin_ref[...]
    half = RT_DK // 2

    def rot(t):
        t1 = t[:, :half]
        t2 = t[:, half:]
        return jnp.concatenate([t1 * cos - t2 * sin, t1 * sin + t2 * cos], axis=1)

    for h in range(RT_HEADS):
        qs = slice(h * RT_DK, (h + 1) * RT_DK)
        vs = slice(h * RT_DV, (h + 1) * RT_DV)
        q = rot(q_ref[:, qs].astype(F32))
        k = rot(k_ref[:, qs].astype(F32)) * (RT_DK ** -0.5)
        v = v_ref[:, vs]
        scores = _bdot_nt(q, k) * dec_ref[h]
        r = r_ref[h]
        o = _bdot(scores, v) + qd_ref[h] * _bdot(q, r)
        r_ref[h] = cd_ref[h] * r + _bdot((k * kd_ref[h]).T, v)
        if add:
            o = o + prev_ref[:, vs].astype(F32)
        o_ref[:, vs] = o.astype(o_ref.dtype)


def _ret_dir(proj, cos, sin, reverse, prev):
    b, l, _ = proj.shape
    c = RT_CHUNK
    nt = l // c
    e = E_WIDTH
    dec, qd, kd, cd = _ret_tables(reverse)

    def rows(ti):
        return (nt - 1 - ti) if reverse else ti

    whole = lambda a: pl.BlockSpec(a.shape, lambda bi, ti: (0, 0, 0))
    in_specs = [
        pl.BlockSpec((None, c, RT_QK), lambda bi, ti: (bi, rows(ti), 0)),
        pl.BlockSpec((None, c, RT_QK), lambda bi, ti: (bi, rows(ti), 1)),
        pl.BlockSpec((None, c, e), lambda bi, ti: (bi, rows(ti), 2 * RT_QK // e)),
        pl.BlockSpec((c, RT_DK // 2), lambda bi, ti: (rows(ti), 0)),
        pl.BlockSpec((c, RT_DK // 2), lambda bi, ti: (rows(ti), 0)),
        whole(dec), whole(qd), whole(kd), whole(cd),
    ]
    args = [proj, proj, proj, cos, sin, dec, qd, kd, cd]
    out_spec = pl.BlockSpec((None, c, e), lambda bi, ti: (bi, rows(ti), 0))
    if prev is not None:
        in_specs.append(out_spec)
        args.append(prev)
    return pl.pallas_call(
        functools.partial(_ret_kernel, add=prev is not None),
        grid=(b, nt),
        in_specs=in_specs,
        out_specs=out_spec,
        out_shape=jax.ShapeDtypeStruct((b, l, e), MIX_DTYPE),
        scratch_shapes=[pltpu.VMEM((RT_HEADS, RT_DK, RT_DV), F32)],
        compiler_params=_cparams(("arbitrary", "arbitrary"), 48),
        name="ret_bwd" if reverse else "ret_fwd",
    )(*args)


def _halo_rows(dtype):
    return SUBLANES * (4 // jnp.dtype(dtype).itemsize)


def _halo_specs(t, w, l, col, order, hr=SUBLANES):
    per = t // hr
    nblk = l // hr
    prev = pl.BlockSpec((None, hr, w),
                        lambda *g: (g[0], jnp.maximum(order(*g) * per - 1, 0), col(*g)))
    nxt = pl.BlockSpec((None, hr, w),
                       lambda *g: (g[0], jnp.minimum((order(*g) + 1) * per, nblk - 1), col(*g)))
    return prev, nxt


def _fill_ext(ext_ref, x_ref, xp_ref, xn_ref, first, last, t):
    hr = xp_ref.shape[0]
    ext_ref[0:SUBLANES, :] = jnp.where(first, 0.0, xp_ref[hr - SUBLANES:hr, :].astype(F32))
    ext_ref[SUBLANES:SUBLANES + t, :] = x_ref[...].astype(F32)
    ext_ref[SUBLANES + t:2 * SUBLANES + t, :] = jnp.where(last, 0.0, xn_ref[0:SUBLANES, :].astype(F32))


def _lru_pitches(t):
    seg = t // SUBLANES
    return seg, seg + 3 * SUBLANES, seg + SUBLANES


def _lru_kernel(*refs, reverse, t, nt, add):
    x_ref, xp_ref, xn_ref, cw_ref, cb_ref, gw_ref, gb_ref, lam_ref = refs[:8]
    prev_ref = refs[8] if add else None
    o_ref, ext_ref, hbuf_ref, carry_ref = refs[-4:]
    ti = pl.program_id(1)
    te = (nt - 1 - ti) if reverse else ti
    seg, pin, pout = _lru_pitches(t)
    nseg = SUBLANES
    left = LRU_CONV // 2

    @pl.when(ti == 0)
    def _():
        carry_ref[...] = jnp.zeros_like(carry_ref)

    neg_lam = -lam_ref[...]
    softplus = jnp.maximum(neg_lam, 0.0) + jnp.log1p(jnp.exp(-jnp.abs(neg_lam)))
    jorder = range(seg - 1, -1, -1) if reverse else range(seg)
    sorder = range(nseg - 1, -1, -1) if reverse else range(nseg)
    edge = 0 if reverse else seg - 1

    for n in range(LRU_BLOCKS):
        sl = slice(n * LRU_BS, (n + 1) * LRU_BS)
        for s in range(nseg):
            r0 = s * seg
            before = (jnp.where(te == 0, 0.0, xp_ref[:, sl]) if s == 0
                      else x_ref[r0 - SUBLANES:r0, sl])
            after = (jnp.where(te == nt - 1, 0.0, xn_ref[:, sl]) if s == nseg - 1
                     else x_ref[r0 + seg:r0 + seg + SUBLANES, sl])
            ext_ref[n, s * pin:s * pin + SUBLANES, :] = before
            ext_ref[n, s * pin + SUBLANES:s * pin + SUBLANES + seg, :] = x_ref[r0:r0 + seg, sl]
            ext_ref[n, s * pin + SUBLANES + seg:s * pin + 2 * SUBLANES + seg, :] = after
        xs = []
        for j in range(seg):
            acc = cb_ref[:, sl]
            for d in range(LRU_CONV):
                acc = acc + cw_ref[d:d + 1, sl] * ext_ref[n, pl.ds(SUBLANES + j + d - left, nseg, stride=pin), :]
            xs.append(acc)
        xn = jnp.concatenate(xs, axis=0)
        gates = _bdot(xn, gw_ref[n]) + gb_ref[n]
        r = _sigmoid(gates[:, :LRU_BS])
        i = _sigmoid(gates[:, LRU_BS:])
        log_a = -LRU_C * r * softplus[:, sl]
        a = jnp.exp(log_a)
        bb = jnp.sqrt(-_expm1(2.0 * log_a)) * (i * xn)
        hs = [None] * seg
        ps = [None] * seg
        h = p = None
        for j in jorder:
            aj = a[j * nseg:(j + 1) * nseg, :]
            bj = bb[j * nseg:(j + 1) * nseg, :]
            h = bj if h is None else aj * h + bj
            p = aj if p is None else aj * p
            hs[j], ps[j] = h, p
        c = carry_ref[:, sl]
        enter = [None] * nseg
        for s in sorder:
            enter[s] = c
            c = hs[edge][s:s + 1, :] + ps[edge][s:s + 1, :] * c
        carry_ref[:, sl] = c
        cin = jnp.concatenate(enter, axis=0)
        for j in range(seg):
            hbuf_ref[n, pl.ds(j, nseg, stride=pout), :] = hs[j] + ps[j] * cin
        for s in range(nseg):
            hn = hbuf_ref[n, s * pout:s * pout + seg, :]
            if add:
                hn = hn + prev_ref[s * seg:(s + 1) * seg, sl].astype(F32)
            o_ref[s * seg:(s + 1) * seg, sl] = hn.astype(o_ref.dtype)


def _lru_dir(proj, conv_w, conv_b, gate_w16, gate_b, lam, reverse, prev):
    b, l, _ = proj.shape
    e = E_WIDTH
    t = min(l, 256)
    nt = l // t

    def order(bi, ti):
        return (nt - 1 - ti) if reverse else ti

    xp_spec, xn_spec = _halo_specs(t, e, l, lambda bi, ti: 0, order)
    in_specs = [
        pl.BlockSpec((None, t, e), lambda bi, ti: (bi, order(bi, ti), 0)), xp_spec, xn_spec,
        pl.BlockSpec((LRU_CONV, e), lambda bi, ti: (0, 0)),
        pl.BlockSpec((1, e), lambda bi, ti: (0, 0)),
        pl.BlockSpec((LRU_BLOCKS, LRU_BS, 2 * LRU_BS), lambda bi, ti: (0, 0, 0)),
        pl.BlockSpec((LRU_BLOCKS, 1, 2 * LRU_BS), lambda bi, ti: (0, 0, 0)),
        pl.BlockSpec((1, e), lambda bi, ti: (0, 0)),
    ]
    args = [proj, proj, proj, conv_w, conv_b.reshape(1, e), gate_w16, gate_b, lam.reshape(1, e)]
    out_spec = pl.BlockSpec((None, t, e), lambda bi, ti: (bi, order(bi, ti), 0))
    if prev is not None:
        in_specs.append(out_spec)
        args.append(prev)
    return pl.pallas_call(
        functools.partial(_lru_kernel, reverse=reverse, t=t, nt=nt, add=prev is not None),
        grid=(b, nt),
        in_specs=in_specs,
        out_specs=out_spec,
        out_shape=jax.ShapeDtypeStruct((b, l, e), MIX_DTYPE),
        scratch_shapes=[pltpu.VMEM((LRU_BLOCKS, SUBLANES * _lru_pitches(t)[1], LRU_BS), F32),
                        pltpu.VMEM((LRU_BLOCKS, SUBLANES * _lru_pitches(t)[2], LRU_BS), F32),
                        pltpu.VMEM((1, e), F32)],
        compiler_params=_cparams(("arbitrary", "arbitrary"), 48),
        name="lru_bwd" if reverse else "lru_fwd",
    )(*args)


def _hy_pre_kernel(x0_ref, x0p_ref, x0n_ref, x1_ref, x1p_ref, x1n_ref, v_ref, vp_ref, vn_ref,
                   z_ref, w0_ref, w1_ref, wv_ref, b0_ref, b1_ref, bv_ref,
                   u_ref, g1_ref, e0_ref, e1_ref, ev_ref, *, t, nt):
    ti = pl.program_id(1)
    first = ti == 0
    last = ti == nt - 1

    def conv(x_ref, xp_ref, xn_ref, ext_ref, w_ref, b_ref):
        _fill_ext(ext_ref, x_ref, xp_ref, xn_ref, first, last, t)
        out = b_ref[...]
        for j in range(3):
            out = out + w_ref[j:j + 1, :] * ext_ref[pl.ds(SUBLANES - 1 + j, t), :]
        return out

    x0 = conv(x0_ref, x0p_ref, x0n_ref, e0_ref, w0_ref, b0_ref)
    x1 = conv(x1_ref, x1p_ref, x1n_ref, e1_ref, w1_ref, b1_ref)
    v = conv(v_ref, vp_ref, vn_ref, ev_ref, wv_ref, bv_ref)
    u_ref[...] = x0 * v
    g1_ref[...] = (x1 * _silu(z_ref[...].astype(F32))).astype(g1_ref.dtype)


def _hy_pre(proj, conv_w, conv_b):
    b, l, _ = proj.shape
    e = E_WIDTH
    w = 512
    t = min(l, 512)
    nt = l // t
    ncol = e // w

    def order(bi, ti, j):
        return ti

    specs, args = [], []
    for s in range(3):
        col = (lambda s: lambda bi, ti, j: s * ncol + j)(s)
        xp, xn = _halo_specs(t, w, l, col, order, _halo_rows(proj.dtype))
        specs += [pl.BlockSpec((None, t, w), (lambda col: lambda bi, ti, j: (bi, ti, col(bi, ti, j)))(col)),
                  xp, xn]
        args += [proj, proj, proj]
    specs.append(pl.BlockSpec((None, t, w), lambda bi, ti, j: (bi, ti, 3 * ncol + j)))
    args.append(proj)
    for s in range(3):
        specs.append(pl.BlockSpec((3, w), (lambda s: lambda bi, ti, j: (0, s * ncol + j))(s)))
        args.append(conv_w)
    cb = conv_b.reshape(1, 3 * e)
    for s in range(3):
        specs.append(pl.BlockSpec((1, w), (lambda s: lambda bi, ti, j: (0, s * ncol + j))(s)))
        args.append(cb)
    out_spec = pl.BlockSpec((None, t, w), lambda bi, ti, j: (bi, ti, j))
    return pl.pallas_call(
        functools.partial(_hy_pre_kernel, t=t, nt=nt),
        grid=(b, nt, ncol),
        in_specs=specs,
        out_specs=[out_spec, out_spec],
        out_shape=[jax.ShapeDtypeStruct((b, l, e), F32), jax.ShapeDtypeStruct((b, l, e), MIX_DTYPE)],
        scratch_shapes=[pltpu.VMEM((t + 2 * SUBLANES, w), F32)] * 3,
        compiler_params=_cparams(("arbitrary", "arbitrary", "arbitrary"), 48),
        name="hy_pre",
    )(*args)


def _hy_filter_kernel(z_ref, w1_ref, b1_ref, w2_ref, b2_ref, fr_ref, wf_ref, wb_ref, dl_ref,
                      sd_ref, a_ref):
    @pl.when(pl.program_id(1) == 0)
    def _():
        fr = fr_ref[...]
        a = jnp.sin(fr * (_bdot(z_ref[...], w1_ref[...]) + b1_ref[...]))
        for j in range(HY_INNER):
            a = jnp.sin(fr * (_bdot(a, w2_ref[j]) + b2_ref[j]))
        a_ref[...] = a

    a = a_ref[...]
    window = jnp.exp(-z_ref[:, 0:1] * dl_ref[...])
    h_fw = _bdot(a, wf_ref[...]) * window
    h_bw = _bdot(a, wb_ref[...]) * window
    sd_ref[0] = h_fw + h_bw
    sd_ref[1] = h_fw - h_bw


def _hy_filters(l, w1, b1, w2, b2, wout16, freq):
    e = E_WIDTH
    kp = LANES
    t = jnp.linspace(0.0, 1.0, l, dtype=F32)[:, None]
    wv = 2.0 * math.pi * jnp.arange(l, dtype=F32)[:, None] / l
    bands = jnp.linspace(1e-4, HY_BANDS - 1, HY_BANDS, dtype=F32)[None]
    z = jnp.concatenate([t, jnp.cos(bands * wv), -jnp.sin(bands * wv),
                         jnp.zeros((l, kp - HY_EMB), F32)], axis=-1)
    w1p = jnp.concatenate([w1, jnp.zeros((kp - HY_EMB, HY_FH), F32)], axis=0)
    max_decay = math.log(HY_TARGET) / HY_FAST_DECAY
    min_decay = math.log(HY_TARGET) / HY_SLOW_DECAY
    deltas = jnp.abs(jnp.linspace(min_decay, max_decay, e, dtype=F32))[None]
    tm = min(l, 512)
    w = 512
    ncol = e // w
    return pl.pallas_call(
        _hy_filter_kernel,
        grid=(l // tm, ncol),
        in_specs=[pl.BlockSpec((tm, kp), lambda i, j: (i, 0)),
                  pl.BlockSpec((kp, HY_FH), lambda i, j: (0, 0)),
                  pl.BlockSpec((1, HY_FH), lambda i, j: (0, 0)),
                  pl.BlockSpec((HY_INNER, HY_FH, HY_FH), lambda i, j: (0, 0, 0)),
                  pl.BlockSpec((HY_INNER, 1, HY_FH), lambda i, j: (0, 0, 0)),
                  pl.BlockSpec((1, HY_FH), lambda i, j: (0, 0)),
                  pl.BlockSpec((HY_FH, w), lambda i, j: (0, j)),
                  pl.BlockSpec((HY_FH, w), lambda i, j: (0, ncol + j)),
                  pl.BlockSpec((1, w), lambda i, j: (0, j))],
        out_specs=pl.BlockSpec((2, tm, w), lambda i, j: (0, i, j)),
        out_shape=jax.ShapeDtypeStruct((2, l, e), F32),
        scratch_shapes=[pltpu.VMEM((tm, HY_FH), F32)],
        compiler_params=_cparams(("arbitrary", "arbitrary"), 32),
        name="hy_filter",
    )(z, w1p, b1.reshape(1, HY_FH), w2, b2.reshape(HY_INNER, 1, HY_FH), freq.reshape(1, HY_FH),
      wout16, wout16, deltas)


FFT_UNROLL = 16


def _fft_dims(l):
    n = 2 * l
    n1 = int(round(math.sqrt(n)))
    assert n1 * n1 == n and n1 % 16 == 0, "sequence length must give a square DFT factorisation"
    return n1, n1


def _fft_pitch(n1):
    return 2 * n1 + SUBLANES


def _fft_tables(l):
    n1, n2 = _fft_dims(l)
    n = n1 * n2
    k1 = np.arange(n1)[:, None]
    m1 = np.arange(n1 // 2)[None, :]
    j2 = np.arange(n2)[:, None, None]
    ang = -2.0 * np.pi * (k1 * m1 / n1)[None] - 2.0 * np.pi * (j2 * k1[None] / n)
    gr, gi = np.cos(ang), np.sin(ang)
    g_fwd = np.concatenate([gr, gi], axis=1)
    g_fwd2 = np.concatenate([np.concatenate([gr, -gi], axis=2), np.concatenate([gi, gr], axis=2)], axis=1)
    ang_i = 2.0 * np.pi * (m1.T * k1.T / n1)[None] + 2.0 * np.pi * (j2 * k1.T[None] / n)
    er, ei = np.cos(ang_i) / n, np.sin(ang_i) / n
    g_inv2 = np.concatenate([np.concatenate([er, -ei], axis=2), np.concatenate([ei, er], axis=2)], axis=1)
    a2 = -2.0 * np.pi * np.arange(n2)[:, None] * np.arange(n2)[None, :] / n2
    fr, fi = np.cos(a2), np.sin(a2)
    f2 = np.block([[fr, -fi], [fi, fr]])
    f2_inv = np.block([[fr, fi], [-fi, fr]])
    f2_half = np.stack([np.concatenate([fr, -fi], axis=1), np.concatenate([fi, fr], axis=1)])
    as16 = lambda a: jnp.asarray(a, F32).astype(BF16)
    return dict(g_fwd=as16(g_fwd), g_fwd2=as16(g_fwd2), g_inv2=as16(g_inv2), f2=as16(f2),
                f2_inv=as16(f2_inv), f2_half=as16(f2_half))


def _fft_stage1(x_ref, g_ref, work_ref, n2_lo, cnt, n1, n2, pitch):
    def body(j, carry):
        jj = n2_lo + j
        xs = x_ref[pl.ds(jj, n1 // 2, stride=n2), :]
        r0 = pl.multiple_of(jj * pitch, SUBLANES)
        work_ref[pl.ds(r0, 2 * n1), :] = _bdot(g_ref[j], xs)
        return carry
    lax.fori_loop(0, cnt, body, 0, unroll=FFT_UNROLL)


def _fft_load_k1(work_ref, k1, n1, n2, pitch):
    br = work_ref[pl.ds(k1, n2, stride=pitch), :]
    bi = work_ref[pl.ds(n1 + k1, n2, stride=pitch), :]
    return jnp.concatenate([br, bi], axis=0)


def _hy_spec_kernel(x_ref, g_ref, f2_ref, t_ref, work_ref, *, n1, n2, nc):
    p = pl.program_id(2)
    pitch = _fft_pitch(n1)
    c2 = n2 // nc
    c1 = n1 // nc

    @pl.when(p < nc)
    def _():
        _fft_stage1(x_ref, g_ref, work_ref, p * c2, c2, n1, n2, pitch)

    @pl.when(p >= nc)
    def _():
        def body(j, carry):
            k1 = (p - nc) * c1 + j
            t_ref[j] = _bdot(f2_ref[...], _fft_load_k1(work_ref, k1, n1, n2, pitch))
            return carry
        lax.fori_loop(0, c1, body, 0, unroll=FFT_UNROLL)


def _fft_nc(l):
    return 4 if l >= 8192 else (2 if l >= 2048 else 1)


def _hy_spectrum(sd, tables):
    _, l, e = sd.shape
    n1, n2 = _fft_dims(l)
    nc = _fft_nc(l)
    g_fwd, f2_half = tables['g_fwd'], tables['f2_half']
    pitch = _fft_pitch(n1)
    return pl.pallas_call(
        functools.partial(_hy_spec_kernel, n1=n1, n2=n2, nc=nc),
        grid=(e // LANES, 2, 2 * nc),
        in_specs=[pl.BlockSpec((None, l, LANES), lambda c, j, p: (j, 0, c)),
                  pl.BlockSpec((n2 // nc, 2 * n1, n1 // 2), lambda c, j, p: (jnp.minimum(p, nc - 1), 0, 0)),
                  pl.BlockSpec((None, n2, 2 * n2), lambda c, j, p: (j, 0, 0))],
        out_specs=pl.BlockSpec((n1 // nc, n2, LANES),
                               lambda c, j, p: (jnp.maximum(p - nc, 0), j, c)),
        out_shape=jax.ShapeDtypeStruct((n1, 2 * n2, e), F32),
        scratch_shapes=[pltpu.VMEM((n2 * pitch, LANES), F32)],
        compiler_params=_cparams(("arbitrary", "arbitrary", "arbitrary"), 48),
        name="hy_spectrum",
    )(sd, g_fwd, f2_half)


def _hy_conv_kernel(u_ref, gf_ref, f2_ref, f2i_ref, t_ref, gi_ref, y_ref, work_ref, *, n1, n2, nc):
    p = pl.program_id(2)
    pitch = _fft_pitch(n1)
    c2 = n2 // nc
    c1 = n1 // nc

    @pl.when(p < nc)
    def _():
        def body(j, carry):
            jj = p * c2 + j
            xs = jnp.concatenate([u_ref[0, pl.ds(jj, n1 // 2, stride=n2), :],
                                  u_ref[1, pl.ds(jj, n1 // 2, stride=n2), :]], axis=0)
            r0 = pl.multiple_of(jj * pitch, SUBLANES)
            work_ref[pl.ds(r0, 2 * n1), :] = _bdot(gf_ref[j], xs)
            return carry
        lax.fori_loop(0, c2, body, 0, unroll=FFT_UNROLL)

    @pl.when(jnp.logical_and(p >= nc, p < 2 * nc))
    def _():
        def body(j, carry):
            k1 = (p - nc) * c1 + j
            x = _bdot(f2_ref[...], _fft_load_k1(work_ref, k1, n1, n2, pitch))
            xr, xi = x[:n2], x[n2:]
            tr, ti = t_ref[j, :n2, :], t_ref[j, n2:, :]
            z = jnp.concatenate([xr * tr - xi * ti, xr * ti + xi * tr], axis=0)
            cmat = _bdot(f2i_ref[...], z)
            work_ref[pl.ds(k1, n2, stride=pitch), :] = cmat[:n2]
            work_ref[pl.ds(n1 + k1, n2, stride=pitch), :] = cmat[n2:]
            return carry
        lax.fori_loop(0, c1, body, 0, unroll=FFT_UNROLL)

    @pl.when(p >= 2 * nc)
    def _():
        def body(j, carry):
            jj = (p - 2 * nc) * c2 + j
            r0 = pl.multiple_of(jj * pitch, SUBLANES)
            d = work_ref[pl.ds(r0, 2 * n1), :]
            y = _bdot(gi_ref[j], d)
            y_ref[0, pl.ds(jj, n1 // 2, stride=n2), :] = y[:n1 // 2]
            y_ref[1, pl.ds(jj, n1 // 2, stride=n2), :] = y[n1 // 2:]
            return carry
        lax.fori_loop(0, c2, body, 0, unroll=FFT_UNROLL)


def _hy_conv(u, spec, tables):
    b, l, e = u.shape
    assert b % 2 == 0, "batch rows are transformed in pairs"
    n1, n2 = _fft_dims(l)
    nc = _fft_nc(l)
    pitch = _fft_pitch(n1)
    clip = lambda v: jnp.clip(v, 0, nc - 1)
    pair_spec = lambda **kw: pl.BlockSpec((2, l, LANES), lambda c, bi, p: (bi, 0, c), **kw)
    return pl.pallas_call(
        functools.partial(_hy_conv_kernel, n1=n1, n2=n2, nc=nc),
        grid=(e // LANES, b // 2, 3 * nc),
        in_specs=[pair_spec(pipeline_mode=pl.Buffered(1) if l >= 8192 else None),
                  pl.BlockSpec((n2 // nc, 2 * n1, n1), lambda c, bi, p: (clip(p), 0, 0)),
                  pl.BlockSpec((2 * n2, 2 * n2), lambda c, bi, p: (0, 0)),
                  pl.BlockSpec((2 * n2, 2 * n2), lambda c, bi, p: (0, 0)),
                  pl.BlockSpec((n1 // nc, 2 * n2, LANES), lambda c, bi, p: (clip(p - nc), 0, c)),
                  pl.BlockSpec((n2 // nc, n1, 2 * n1), lambda c, bi, p: (clip(p - 2 * nc), 0, 0))],
        out_specs=pair_spec(pipeline_mode=pl.Buffered(1)),
        out_shape=jax.ShapeDtypeStruct((b, l, e), F32),
        scratch_shapes=[pltpu.VMEM((n2 * pitch, LANES), F32)],
        compiler_params=_cparams(("arbitrary", "arbitrary", "arbitrary"), 56),
        name="hy_conv",
    )(u, tables['g_fwd2'], tables['f2'], tables['f2_inv'], spec, tables['g_inv2'])


def _prep_weights(p):
    c16 = lambda a: a.astype(BF16)
    gw = p['lru_gate_w'][0]
    gw = jnp.concatenate([gw[:, 0], gw[:, 1]], axis=-1)
    gb = p['lru_gate_b'][0].reshape(2, 2, LRU_BLOCKS, 1, LRU_BS)
    gb = jnp.concatenate([gb[:, 0], gb[:, 1]], axis=-1)
    lb = p['hg_lb'].astype(F32)
    return dict(
        ada_w=c16(p['ada_w']), hg_w_in=c16(p['hg_w_in'][0]), hg_w_out=c16(p['hg_w_out'][0]),
        hy_w_in=c16(p['hy_w_in'][0]), hy_w_out=c16(p['hy_w_out'][0]), hy_f_wout=c16(p['hy_f_wout'][0]),
        rt_w_in=c16(p['rt_w_in'][0]), rt_w_out=c16(p['rt_w_out'][0]),
        lru_w_in=c16(p['lru_w_in'][0]), lru_w_out=c16(p['lru_w_out'][0]),
        lru_gate_w=c16(gw), lru_gate_b=gb, hg_lb=lb,
        hg_norm_g=jnp.tile(p['hg_norm_g'][0], HG_HEADS))


def _trunk(x, mod, p, w):
    b, l, d = x.shape
    e = E_WIDTH
    x = x.astype(F32)
    zero_bias = lambda n: jnp.zeros((n,), F32)

    def split(layer):
        m = mod[layer]
        return m[:, :d], m[:, d:2 * d], m[:, 2 * d:]

    shift, scale, gate = split(0)
    proj = _in_proj(x, p['norm_g'][0], scale, shift, w['hg_w_in'], zero_bias(5 * e))
    o = _hgrn2_dir(proj, w['hg_lb'], False, None)
    o = _hgrn2_dir(proj, w['hg_lb'], True, o)
    x = _out_proj(_mix_hgrn2, [o, proj, w['hg_norm_g'].reshape(1, e)],
                  lambda tm: [_row_spec(tm, e, 0), _row_spec(tm, e, 4), _vec_spec(e)],
                  w['hg_w_out'], x, gate)

    shift, scale, gate = split(1)
    proj = _in_proj(x, p['norm_g'][1], scale, shift, w['hy_w_in'], p['hy_b_in'][0], PROJ_DTYPE_NARROW)
    u, g1 = _hy_pre(proj, p['hy_conv_w'][0], p['hy_conv_b'][0])
    tables = _fft_tables(l)
    sd = _hy_filters(l, p['hy_f_w1'][0], p['hy_f_b1'][0], p['hy_f_w2'][0], p['hy_f_b2'][0],
                     w['hy_f_wout'], p['hy_f_freq'][0])
    spec = _hy_spectrum(sd, tables)
    yc = _hy_conv(u, spec, tables)
    x = _out_proj(_mix_hyena, [yc, u, g1, p['hy_skip'][0].reshape(1, e)],
                  lambda tm: [_row_spec(tm, e, 0)] * 3 + [_vec_spec(e)],
                  w['hy_w_out'], x, gate)

    shift, scale, gate = split(2)
    proj = _in_proj(x, p['norm_g'][2], scale, shift, w['rt_w_in'], zero_bias(2 * RT_QK + 2 * e),
                    PROJ_DTYPE_NARROW)
    cos, sin = _rope_tables(l)
    o = _ret_dir(proj, cos, sin, False, None)
    o = _ret_dir(proj, cos, sin, True, o)
    x = _out_proj(_mix_retention, [o, proj, p['rt_gn_g'][0].reshape(1, e)],
                  lambda tm: [_row_spec(tm, e, 0), _row_spec(tm, e, 2), _vec_spec(e)],
                  w['rt_w_out'], x, gate)

    shift, scale, gate = split(3)
    proj = _in_proj(x, p['norm_g'][3], scale, shift, w['lru_w_in'], zero_bias(2 * e))
    y = None
    for dirn in range(2):
        y = _lru_dir(proj, p['lru_conv_w'][0], p['lru_conv_b'][0], w['lru_gate_w'][dirn],
                     w['lru_gate_b'][dirn], p['lru_lambda'][0][dirn], dirn == 1, y)
    return _out_proj(_mix_lru, [y, proj], lambda tm: [_row_spec(tm, e, 0), _row_spec(tm, e, 1)],
                     w['lru_w_out'], x, gate, final_g=p['final_g'])


def kernel(x_prompt, x_sample, c_prompt, c_sample, ada_w, ada_b, norm_g, final_g, hg_lb, hg_w_in, hg_norm_g, hg_w_out, hy_w_in, hy_b_in, hy_conv_w, hy_conv_b, hy_f_w1, hy_f_b1, hy_f_w2, hy_f_b2, hy_f_wout, hy_f_freq, hy_skip, hy_w_out, rt_w_in, rt_gn_g, rt_w_out, lru_w_in, lru_conv_w, lru_conv_b, lru_gate_w, lru_gate_b, lru_lambda, lru_w_out):
    p = dict(ada_w=ada_w, ada_b=ada_b, norm_g=norm_g, final_g=final_g, hg_lb=hg_lb, hg_w_in=hg_w_in,
             hg_norm_g=hg_norm_g, hg_w_out=hg_w_out, hy_w_in=hy_w_in, hy_b_in=hy_b_in,
             hy_conv_w=hy_conv_w, hy_conv_b=hy_conv_b, hy_f_w1=hy_f_w1, hy_f_b1=hy_f_b1,
             hy_f_w2=hy_f_w2, hy_f_b2=hy_f_b2, hy_f_wout=hy_f_wout, hy_f_freq=hy_f_freq,
             hy_skip=hy_skip, hy_w_out=hy_w_out, rt_w_in=rt_w_in, rt_gn_g=rt_gn_g, rt_w_out=rt_w_out,
             lru_w_in=lru_w_in, lru_conv_w=lru_conv_w, lru_conv_b=lru_conv_b, lru_gate_w=lru_gate_w,
             lru_gate_b=lru_gate_b, lru_lambda=lru_lambda, lru_w_out=lru_w_out)
    w = _prep_weights(p)
    bp, bs = c_prompt.shape[0], c_sample.shape[0]
    rows = -(-(bp + bs) // SUBLANES) * SUBLANES
    c_all = jnp.concatenate([c_prompt, c_sample, jnp.zeros((rows - bp - bs, D_MODEL), F32)], axis=0)
    mod = _adaln(c_all.astype(F32), w['ada_w'], ada_b)
    y_prompt = _trunk(x_prompt, mod[:, :bp], p, w).astype(x_prompt.dtype)
    y_sample = _trunk(x_sample, mod[:, bp:bp + bs], p, w).astype(x_sample.dtype)
    return (y_prompt, y_sample)
```
